```python
import jax, jax.numpy as jnp
from jax import lax
import numpy as np

D_MODEL = 1024
BATCH = 4
SEQ = 8192
DEPTH = 4

N_MIXERS = 2
N_MLA_LAYERS = (DEPTH + 1) // 2
N_HGRN_LAYERS = DEPTH // 2
N_NORMS_PER_LAYER = 6
D_FF = 2816
MLA_HEADS = 8
QK_NOPE_DIM = 128
QK_ROPE_DIM = 64
V_HEAD_DIM = 128
Q_LORA_RANK = 256
KV_LORA_RANK = 128
ROPE_THETA = 10000.0
Q_BLOCK = 128
HGRN_HEADS = 8
HGRN_F_DIM = D_MODEL // HGRN_HEADS
HGRN_I_DIM = D_MODEL // HGRN_HEADS
CHUNK = 64
EPS = 1e-6
MAX_POS_OFFSET = 4096

kernel_name = "hybrid_mla_hgrn2_macaron_encoder"


def rmsnorm(x, gain):
    xf = x.astype(jnp.float32)
    y = xf * lax.rsqrt(jnp.mean(xf * xf, axis=-1, keepdims=True) + EPS)
    return (y * gain.astype(jnp.float32)).astype(x.dtype)


def swiglu(h, w_in, w_out):
    gate, up = jnp.split(h @ w_in, 2, axis=-1)
    return (jax.nn.silu(gate) * up) @ w_out


def rope_tables(positions, dtype):
    inv_freq = ROPE_THETA ** (-jnp.arange(0, QK_ROPE_DIM, 2, dtype=jnp.float32) / QK_ROPE_DIM)
    ang = positions.astype(jnp.float32)[..., None] * inv_freq
    return jnp.cos(ang).astype(dtype), jnp.sin(ang).astype(dtype)


def rope(t, cos, sin):
    t1, t2 = jnp.split(t, 2, axis=-1)
    return jnp.concatenate([t1 * cos - t2 * sin, t1 * sin + t2 * cos], axis=-1)


def mla(h, positions, w_in, q_norm, kv_norm, w_uq, w_ukv, w_o):
    B, S, _ = h.shape
    c_q, c_kv, k_r = jnp.split(h @ w_in, [Q_LORA_RANK, Q_LORA_RANK + KV_LORA_RANK], axis=-1)
    c_q = rmsnorm(c_q, q_norm)
    c_kv = rmsnorm(c_kv, kv_norm)
    q = (c_q @ w_uq).reshape(B, S, MLA_HEADS, QK_NOPE_DIM + QK_ROPE_DIM)
    q_nope, q_rope = jnp.split(q, [QK_NOPE_DIM], axis=-1)
    kv = (c_kv @ w_ukv).reshape(B, S, MLA_HEADS, QK_NOPE_DIM + V_HEAD_DIM)
    k_nope, v = jnp.split(kv, [QK_NOPE_DIM], axis=-1)
    cos, sin = rope_tables(positions, h.dtype)
    q_rope = rope(q_rope, cos[:, :, None], sin[:, :, None])
    k_rope = rope(k_r, cos, sin)
    scale = (QK_NOPE_DIM + QK_ROPE_DIM) ** -0.5
    nb = S // Q_BLOCK

    def to_blocks(t):
        return t.reshape(B, nb, Q_BLOCK, *t.shape[2:]).swapaxes(0, 1)

    def attend(blk):
        qn, qr = blk
        s = (jnp.einsum('bqhd,bkhd->bhqk', qn, k_nope)
             + jnp.einsum('bqhr,bkr->bhqk', qr, k_rope))
        p = jax.nn.softmax(s.astype(jnp.float32) * scale, axis=-1)
        return jnp.einsum('bhqk,bkhd->bqhd', p.astype(v.dtype), v)

    o = lax.map(attend, (to_blocks(q_nope), to_blocks(q_rope)))
    o = o.swapaxes(0, 1).reshape(B, S, MLA_HEADS * V_HEAD_DIM)
    return o @ w_o


def gated_chunk_scan(q, k, v, log_f):
    BB, S, H, DK = q.shape
    DV = v.shape[-1]
    n = S // CHUNK

    def to_chunks(t):
        return t.reshape(BB, n, CHUNK, H, t.shape[-1]).transpose(1, 0, 3, 2, 4)

    qc, kc, vc = to_chunks(q), to_chunks(k), to_chunks(v)
    bc = jnp.cumsum(to_chunks(log_f), axis=-2)
    lower = jnp.tril(jnp.ones((CHUNK, CHUNK), dtype=bool))[:, :, None]

    def step(state, inp):
        qt, kt, vt, bt = inp
        diff = jnp.where(lower, bt[:, :, :, None, :] - bt[:, :, None, :, :], -jnp.inf)
        scores = jnp.einsum('bhtd,bhsd,bhtsd->bhts', qt, kt, jnp.exp(diff))
        o = scores @ vt + jnp.einsum('bhtd,bhdv->bhtv', qt * jnp.exp(bt), state)
        b_last = bt[:, :, -1:, :]
        state = (jnp.exp(b_last[:, :, 0, :, None]) * state
                 + jnp.einsum('bhsd,bhsv->bhdv', kt * jnp.exp(b_last - bt), vt))
        return state, o

    state0 = jnp.zeros((BB, H, DK, DV), jnp.float32)
    _, o = lax.scan(step, state0, (qc, kc, vc, bc))
    return o.transpose(1, 0, 3, 2, 4).reshape(BB, S, H, DV)


def hgrn2(h, layer_idx, w_in, lower_bound, out_norm, w_o):
    B, S, _ = h.shape
    q, z_fw, z_bw, i, g = jnp.split(h @ w_in, 5, axis=-1)
    sm = jax.nn.softmax(lower_bound.astype(jnp.float32), axis=1)
    lb = (jnp.cumsum(sm, axis=1) - sm[:, :1])[:, layer_idx]
    lbb = lb[:, None, None, :]
    z = jnp.stack([z_fw, z_bw]).astype(jnp.float32)
    log_f = jnp.logaddexp(jnp.log(lbb), jnp.log1p(-lbb) + jax.nn.log_sigmoid(z))
    k = -jnp.expm1(log_f)
    q = jax.nn.silu(q).astype(jnp.float32)
    i = i.astype(jnp.float32)

    def heads(t):
        return t.reshape(*t.shape[:-1], HGRN_HEADS, t.shape[-1] // HGRN_HEADS)

    def flip(t):
        return t[:, ::-1]

    qq = jnp.concatenate([q, flip(q)], axis=0)
    ii = jnp.concatenate([i, flip(i)], axis=0)
    ff = jnp.concatenate([log_f[0], flip(log_f[1])], axis=0)
    kk = jnp.concatenate([k[0], flip(k[1])], axis=0)
    o = gated_chunk_scan(heads(qq), heads(kk), heads(ii), heads(ff))
    o = o[:B] + flip(o[B:])
    o = rmsnorm(o, out_norm.reshape(HGRN_HEADS, HGRN_I_DIM)) * heads(jax.nn.silu(g.astype(jnp.float32)))
    return o.reshape(B, S, D_MODEL).astype(h.dtype) @ w_o


def setup_inputs(seed: int = 0) -> dict:
    key = jax.random.key(seed)
    ks = jax.random.split(key, 15)

    def w(k, shape, fan_in):
        return jax.random.normal(k, shape, jnp.float32) * fan_in ** -0.5

    def gain(k, shape):
        return 1.0 + 0.05 * jax.random.normal(k, shape, jnp.float32)

    x = jax.random.normal(ks[0], (BATCH, SEQ, D_MODEL), jnp.float32)
    positions = (jnp.arange(SEQ, dtype=jnp.int32)[None, :]
                 + jax.random.randint(ks[1], (BATCH, 1), 0, MAX_POS_OFFSET, dtype=jnp.int32))
    norm_gains = gain(ks[2], (DEPTH, N_NORMS_PER_LAYER, D_MODEL))
    ffn_w_in = w(ks[3], (DEPTH, 2, D_MODEL, 2 * D_FF), D_MODEL)
    ffn_w_out = w(ks[4], (DEPTH, 2, D_FF, D_MODEL), D_FF)
    mla_w_in = w(ks[5], (N_MLA_LAYERS, D_MODEL, Q_LORA_RANK + KV_LORA_RANK + QK_ROPE_DIM), D_MODEL)
    mla_q_norm = gain(ks[6], (N_MLA_LAYERS, Q_LORA_RANK))
    mla_kv_norm = gain(ks[7], (N_MLA_LAYERS, KV_LORA_RANK))
    mla_w_uq = w(ks[8], (N_MLA_LAYERS, Q_LORA_RANK, MLA_HEADS * (QK_NOPE_DIM + QK_ROPE_DIM)), Q_LORA_RANK)
    mla_w_ukv = w(ks[9], (N_MLA_LAYERS, KV_LORA_RANK, MLA_HEADS * (QK_NOPE_DIM + V_HEAD_DIM)), KV_LORA_RANK)
    mla_w_o = w(ks[10], (N_MLA_LAYERS, MLA_HEADS * V_HEAD_DIM, D_MODEL), MLA_HEADS * V_HEAD_DIM)
    hgrn_w_in = w(ks[11], (N_HGRN_LAYERS, D_MODEL, 5 * D_MODEL), D_MODEL)
    hgrn_lower_bound = 0.1 * jax.random.normal(ks[12], (2, DEPTH, HGRN_HEADS * HGRN_F_DIM), jnp.float32)
    hgrn_out_norm = gain(ks[13], (N_HGRN_LAYERS, D_MODEL))
    hgrn_w_o = w(ks[14], (N_HGRN_LAYERS, D_MODEL, D_MODEL), D_MODEL)
    return {"x": x, "positions": positions, "norm_gains": norm_gains,
            "ffn_w_in": ffn_w_in, "ffn_w_out": ffn_w_out,
            "mla_w_in": mla_w_in, "mla_q_norm": mla_q_norm, "mla_kv_norm": mla_kv_norm,
            "mla_w_uq": mla_w_uq, "mla_w_ukv": mla_w_ukv, "mla_w_o": mla_w_o,
            "hgrn_w_in": hgrn_w_in, "hgrn_lower_bound": hgrn_lower_bound,
            "hgrn_out_norm": hgrn_out_norm, "hgrn_w_o": hgrn_w_o}


def reference(x, positions, norm_gains, ffn_w_in, ffn_w_out, mla_w_in, mla_q_norm, mla_kv_norm,
              mla_w_uq, mla_w_ukv, mla_w_o, hgrn_w_in, hgrn_lower_bound, hgrn_out_norm, hgrn_w_o):
    for l in range(DEPTH):
        g = norm_gains[l]
        x = x + 0.5 * rmsnorm(swiglu(rmsnorm(x, g[0]), ffn_w_in[l, 0], ffn_w_out[l, 0]), g[1])
        h = rmsnorm(x, g[2])
        j = l // N_MIXERS
        if l % N_MIXERS == 0:
            m = mla(h, positions, mla_w_in[j], mla_q_norm[j], mla_kv_norm[j],
                    mla_w_uq[j], mla_w_ukv[j], mla_w_o[j])
        else:
            m = hgrn2(h, l, hgrn_w_in[j], hgrn_lower_bound, hgrn_out_norm[j], hgrn_w_o[j])
        x = x + rmsnorm(m, g[3])
        x = x + 0.5 * rmsnorm(swiglu(rmsnorm(x, g[4]), ffn_w_in[l, 1], ffn_w_out[l, 1]), g[5])
    return x
```

```python
import functools
import math

import numpy as np
import jax
import jax.numpy as jnp
from jax import lax
from jax.experimental import pallas as pl
from jax.experimental.pallas import tpu as pltpu

EPS = 1e-6
ROPE_THETA = 10000.0
MLA_HEADS = 8
QK_NOPE_DIM = 128
QK_ROPE_DIM = 64
V_HEAD_DIM = 128
Q_LORA_RANK = 256
KV_LORA_RANK = 128
HGRN_HEADS = 8
N_MIXERS = 2

LANES = 128
QK_PAD_DIM = 256
SCAN_CHUNK = 128
SCAN_LEVELS = 7
VMEM_LIMIT = 56 * 1024 * 1024

BF16 = jnp.bfloat16
F32 = jnp.float32

_NT = (((1,), (1,)), ((), ()))
_TN = (((0,), (0,)), ((), ()))


def _params(*semantics):
    return pltpu.CompilerParams(dimension_semantics=semantics, vmem_limit_bytes=VMEM_LIMIT)


def _resident(shape):
    nd = len(shape)
    return pl.BlockSpec(shape, lambda *_: (0,) * nd, pipeline_mode=pl.Buffered(1))


def _rms(y, gain):
    return y * lax.rsqrt(jnp.mean(y * y, axis=-1, keepdims=True) + EPS) * gain


def _silu(t):
    return t * (1.0 / (1.0 + jnp.exp(-t)))


FFN_CHUNK = 256


def _ffn_kernel(x_ref, gin_ref, gout_ref, wgu_ref, wout_ref, o_ref, a_scr, *, n_chunks):
    x = x_ref[...]
    h = _rms(x, gin_ref[...]).astype(BF16)
    for c in range(n_chunks):
        gu = jnp.dot(h, wgu_ref[:, c * 2 * FFN_CHUNK:(c + 1) * 2 * FFN_CHUNK],
                     preferred_element_type=F32)
        gate = gu[:, :FFN_CHUNK]
        up = gu[:, FFN_CHUNK:]
        a_scr[:, c * FFN_CHUNK:(c + 1) * FFN_CHUNK] = (_silu(gate) * up).astype(BF16)
    y = jnp.dot(a_scr[...], wout_ref[...], preferred_element_type=F32)
    o_ref[...] = x + 0.5 * _rms(y, gout_ref[...])


def _ffn(x, g_in, g_out, w_in, w_out, *, tm=512):
    rows, d = x.shape
    d_ff = w_out.shape[0]
    n_chunks = d_ff // FFN_CHUNK
    wg = w_in[:, :d_ff].reshape(d, n_chunks, FFN_CHUNK)
    wu = w_in[:, d_ff:].reshape(d, n_chunks, FFN_CHUNK)
    wgu = jnp.concatenate([wg, wu], axis=2).reshape(d, 2 * d_ff).astype(BF16)
    return pl.pallas_call(
        functools.partial(_ffn_kernel, n_chunks=n_chunks),
        out_shape=jax.ShapeDtypeStruct((rows, d), F32),
        grid=(rows // tm,),
        in_specs=[
            pl.BlockSpec((tm, d), lambda i: (i, 0)),
            _resident((1, d)),
            _resident((1, d)),
            _resident((d, 2 * d_ff)),
            _resident((d_ff, d)),
        ],
        out_specs=pl.BlockSpec((tm, d), lambda i: (i, 0)),
        scratch_shapes=[pltpu.VMEM((tm, d_ff), BF16)],
        compiler_params=_params("parallel"),
        name="ffn",
    )(x, g_in.reshape(1, d), g_out.reshape(1, d), wgu, w_out.astype(BF16))


def _mla_out_kernel(x_ref, o_ref_in, wo_ref, g_ref, out_ref):
    m = jnp.dot(o_ref_in[...], wo_ref[...], preferred_element_type=F32)
    out_ref[...] = x_ref[...] + _rms(m, g_ref[...])


def _mla_out(x, o, w_o, gain, *, tm=512):
    rows, d = x.shape
    return pl.pallas_call(
        _mla_out_kernel,
        out_shape=jax.ShapeDtypeStruct((rows, d), F32),
        grid=(rows // tm,),
        in_specs=[
            pl.BlockSpec((tm, d), lambda i: (i, 0)),
            pl.BlockSpec((tm, o.shape[1]), lambda i: (i, 0)),
            _resident(w_o.shape),
            _resident((1, d)),
        ],
        out_specs=pl.BlockSpec((tm, d), lambda i: (i, 0)),
        compiler_params=_params("parallel"),
        name="mla_out",
    )(x, o, w_o.astype(BF16), gain.reshape(1, d))


def _mla_proj_kernel(x_ref, pos_ref, g_ref, win_ref, qn_ref, kvn_ref, wuq_ref, wuqs_ref, wukv_ref,
                     invf_ref, sign_ref, q_ref, k_ref, v_ref, *, q_scale):
    h = _rms(x_ref[...], g_ref[...]).astype(BF16)
    p = jnp.dot(h, win_ref[...], preferred_element_type=F32)
    cq = _rms(p[:, :Q_LORA_RANK], qn_ref[...] * q_scale).astype(BF16)
    ckv = _rms(p[:, Q_LORA_RANK:Q_LORA_RANK + KV_LORA_RANK], kvn_ref[...]).astype(BF16)
    kr = p[:, Q_LORA_RANK + KV_LORA_RANK:Q_LORA_RANK + KV_LORA_RANK + LANES]
    kr_sw = p[:, Q_LORA_RANK + KV_LORA_RANK + LANES:]
    ang = pos_ref[...].astype(F32) * invf_ref[...]
    cos = jnp.cos(ang)
    sin = jnp.sin(ang) * sign_ref[...]
    k_rope = (kr * cos + kr_sw * sin).astype(BF16)
    q_all = jnp.dot(cq, wuq_ref[...], preferred_element_type=F32)
    q_sw = jnp.dot(cq, wuqs_ref[...], preferred_element_type=F32)
    kv = jnp.dot(ckv, wukv_ref[...], preferred_element_type=F32)
    for hd in range(MLA_HEADS):
        b0 = hd * QK_PAD_DIM
        q_ref[:, b0:b0 + LANES] = q_all[:, b0:b0 + LANES].astype(BF16)
        q_ref[:, b0 + LANES:b0 + 2 * LANES] = (
            q_all[:, b0 + LANES:b0 + 2 * LANES] * cos + q_sw[:, hd * LANES:(hd + 1) * LANES] * sin
        ).astype(BF16)
        k_ref[:, b0:b0 + LANES] = kv[:, b0:b0 + LANES].astype(BF16)
        k_ref[:, b0 + LANES:b0 + 2 * LANES] = k_rope
        v_ref[:, hd * LANES:(hd + 1) * LANES] = kv[:, b0 + LANES:b0 + 2 * LANES].astype(BF16)


def _pad_rope_cols(w, swap):
    half = QK_ROPE_DIM // 2
    a, b = w[:, :half], w[:, half:]
    if swap:
        a, b = b, a
    return jnp.concatenate([a, b, jnp.zeros((w.shape[0], LANES - QK_ROPE_DIM), w.dtype)], axis=1)


def _mla_proj(x, pos, gain, w_in, q_norm, kv_norm, w_uq, w_ukv, *, tm=512):
    rows, d = x.shape
    lat = Q_LORA_RANK + KV_LORA_RANK
    w_kr = w_in[:, lat:]
    win_ext = jnp.concatenate(
        [w_in[:, :lat], _pad_rope_cols(w_kr, False), _pad_rope_cols(w_kr, True)], axis=1).astype(BF16)
    wq = w_uq.reshape(Q_LORA_RANK, MLA_HEADS, QK_NOPE_DIM + QK_ROPE_DIM)
    wuq_ext = jnp.concatenate(
        [jnp.concatenate([wq[:, hd, :QK_NOPE_DIM], _pad_rope_cols(wq[:, hd, QK_NOPE_DIM:], False)], axis=1)
         for hd in range(MLA_HEADS)], axis=1).astype(BF16)
    wuq_sw = jnp.concatenate(
        [_pad_rope_cols(wq[:, hd, QK_NOPE_DIM:], True) for hd in range(MLA_HEADS)], axis=1).astype(BF16)
    half = QK_ROPE_DIM // 2
    inv_freq = ROPE_THETA ** (-np.arange(0, QK_ROPE_DIM, 2, dtype=np.float32) / QK_ROPE_DIM)
    invf = np.zeros((1, LANES), np.float32)
    invf[0, :half] = inv_freq
    invf[0, half:2 * half] = inv_freq
    sign = np.zeros((1, LANES), np.float32)
    sign[0, :half] = -1.0
    sign[0, half:2 * half] = 1.0
    q_scale = (QK_NOPE_DIM + QK_ROPE_DIM) ** -0.5 * math.log2(math.e)
    hq = MLA_HEADS * QK_PAD_DIM
    return pl.pallas_call(
        functools.partial(_mla_proj_kernel, q_scale=q_scale),
        out_shape=(jax.ShapeDtypeStruct((rows, hq), BF16),
                   jax.ShapeDtypeStruct((rows, hq), BF16),
                   jax.ShapeDtypeStruct((rows, MLA_HEADS * V_HEAD_DIM), BF16)),
        grid=(rows // tm,),
        in_specs=[
            pl.BlockSpec((tm, d), lambda i: (i, 0)),
            pl.BlockSpec((tm, 1), lambda i: (i, 0)),
            _resident((1, d)),
            _resident(win_ext.shape),
            _resident((1, Q_LORA_RANK)),
            _resident((1, KV_LORA_RANK)),
            _resident(wuq_ext.shape),
            _resident(wuq_sw.shape),
            _resident(w_ukv.shape),
            _resident((1, LANES)),
            _resident((1, LANES)),
        ],
        out_specs=(pl.BlockSpec((tm, hq), lambda i: (i, 0)),
                   pl.BlockSpec((tm, hq), lambda i: (i, 0)),
                   pl.BlockSpec((tm, MLA_HEADS * V_HEAD_DIM), lambda i: (i, 0))),
        compiler_params=_params("parallel"),
        name="mla_proj",
    )(x, pos.reshape(rows, 1), gain.reshape(1, d), win_ext, q_norm.reshape(1, -1), kv_norm.reshape(1, -1),
      wuq_ext, wuq_sw, w_ukv.astype(BF16), jnp.asarray(invf), jnp.asarray(sign))


def _attn_kernel(q_ref, k_ref, v_ref, o_ref, *, tk):
    q = q_ref[...]
    tq = q.shape[0]
    n_kv = k_ref.shape[0] // tk

    def body(j, carry):
        m, l, acc = carry
        start = pl.multiple_of(j * tk, tk)
        k = k_ref[pl.ds(start, tk), :]
        v = v_ref[pl.ds(start, tk), :]
        s = lax.dot_general(q, k, _NT, preferred_element_type=F32)
        m_new = jnp.maximum(m, jnp.max(s, axis=-1, keepdims=True))
        alpha = jnp.exp2(m - m_new)
        p = jnp.exp2(s - m_new)
        l = alpha * l + jnp.sum(p, axis=-1, keepdims=True)
        acc = alpha * acc + jnp.dot(p.astype(BF16), v, preferred_element_type=F32)
        return m_new, l, acc

    m0 = jnp.full((tq, 1), -jnp.inf, F32)
    l0 = jnp.zeros((tq, 1), F32)
    acc0 = jnp.zeros((tq, V_HEAD_DIM), F32)
    _, l, acc = lax.fori_loop(0, n_kv, body, (m0, l0, acc0))
    o_ref[...] = (acc / l).astype(o_ref.dtype)


def _attention(q, k, v, batch, seq, *, tq=256, tk=1024):
    rows = q.shape[0]
    nq = seq // tq
    tk = min(tk, seq)
    return pl.pallas_call(
        functools.partial(_attn_kernel, tk=tk),
        out_shape=jax.ShapeDtypeStruct((rows, MLA_HEADS * V_HEAD_DIM), BF16),
        grid=(batch, MLA_HEADS, nq),
        in_specs=[
            pl.BlockSpec((tq, QK_PAD_DIM), lambda b, h, i: (b * nq + i, h)),
            pl.BlockSpec((seq, QK_PAD_DIM), lambda b, h, i: (b, h)),
            pl.BlockSpec((seq, V_HEAD_DIM), lambda b, h, i: (b, h)),
        ],
        out_specs=pl.BlockSpec((tq, V_HEAD_DIM), lambda b, h, i: (b * nq + i, h)),
        compiler_params=_params("parallel", "parallel", "parallel"),
        name="mla_attn",
    )(q, k, v)


def _mla_layer(x, pos, batch, seq, g_pre, g_post, w_in, q_norm, kv_norm, w_uq, w_ukv, w_o):
    q, k, v = _mla_proj(x, pos, g_pre, w_in, q_norm, kv_norm, w_uq, w_ukv)
    o = _attention(q, k, v, batch, seq)
    return _mla_out(x, o, w_o, g_post)


def _hgrn_proj_kernel(x_ref, g_ref, w_ref, o_ref, h_scr):
    @pl.when(pl.program_id(1) == 0)
    def _():
        h_scr[...] = _rms(x_ref[...], g_ref[...]).astype(BF16)

    o_ref[...] = jnp.dot(h_scr[...], w_ref[...], preferred_element_type=F32)


def _hgrn_proj(x, gain, w_in, *, tm=1024):
    rows, d = x.shape
    n_out = w_in.shape[1]
    return pl.pallas_call(
        _hgrn_proj_kernel,
        out_shape=jax.ShapeDtypeStruct((rows, n_out), F32),
        grid=(rows // tm, n_out // d),
        in_specs=[
            pl.BlockSpec((tm, d), lambda i, j: (i, 0)),
            _resident((1, d)),
            pl.BlockSpec((d, d), lambda i, j: (0, j)),
        ],
        out_specs=pl.BlockSpec((tm, d), lambda i, j: (i, j)),
        scratch_shapes=[pltpu.VMEM((tm, d), BF16)],
        compiler_params=_params("parallel", "arbitrary"),
        name="hgrn_proj",
    )(x, gain.reshape(1, d), w_in.astype(BF16))


SCAN_EXP_BLOCKS = SCAN_LEVELS + 2
SCAN_TOTAL_ROWS = 16


def _scan_exponent_matrix():
    c = SCAN_CHUNK
    out = np.zeros((2, SCAN_EXP_BLOCKS * c + SCAN_TOTAL_ROWS, c), np.float32)
    r = np.arange(c)
    for t in range(c):
        for lv in range(SCAN_LEVELS):
            mid = ((t >> lv) | 1) << lv
            if (t >> lv) & 1:
                out[0, lv * c + t, (r >= mid) & (r <= t)] = 1.0
                out[1, lv * c + t, (r >= mid) & (r < t)] = 1.0
            else:
                out[0, lv * c + t, (r > t) & (r < mid)] = 1.0
                out[1, lv * c + t, (r >= t) & (r < mid)] = 1.0
        out[0, SCAN_LEVELS * c + t, r <= t] = 1.0
        out[1, SCAN_LEVELS * c + t, r >= t] = 1.0
        out[0, (SCAN_LEVELS + 1) * c + t, r > t] = 1.0
        out[1, (SCAN_LEVELS + 1) * c + t, r < t] = 1.0
    out[:, SCAN_EXP_BLOCKS * c:, :] = 1.0
    return np.concatenate([out, out], axis=2)


def _hgrn_scan_kernel(q_ref, z_ref, i_ref, lbp_ref, a_ref, o_ref, st_scr, *, layer_idx):
    d = pl.program_id(1)
    c = SCAN_CHUNK

    @pl.when(pl.program_id(2) == 0)
    def _():
        st_scr[...] = jnp.zeros_like(st_scr)

    lbp = lbp_ref[...]
    e = jnp.exp(lbp - jnp.max(lbp, axis=0, keepdims=True))
    sm = e / jnp.sum(e, axis=0, keepdims=True)
    lb = jnp.sum(sm[1:layer_idx + 1], axis=0, keepdims=True)

    sg = 1.0 / (1.0 + jnp.exp(-z_ref[...]))
    f = lb + (1.0 - lb) * sg
    kk = (1.0 - lb) * (1.0 - sg)
    logf = jnp.log(f)
    hi = logf.astype(BF16)
    lo = (logf - hi.astype(F32)).astype(BF16)
    expo = jnp.dot(a_ref[...], jnp.concatenate([hi, lo], axis=0), preferred_element_type=F32)
    w = jnp.exp(expo)

    qs = _silu(q_ref[...])
    v = i_ref[...].astype(BF16)
    dqk = qs - kk
    row = lax.broadcasted_iota(jnp.int32, (c, 1), 0)
    xs = []
    for lv in range(SCAN_LEVELS):
        is_q = (((row >> lv) & 1) != d).astype(F32)
        xs.append((w[lv * c:(lv + 1) * c] * (kk + is_q * dqk)).astype(BF16))
    q_in = (qs * w[SCAN_LEVELS * c:(SCAN_LEVELS + 1) * c]).astype(BF16)
    k_out = (kk * w[(SCAN_LEVELS + 1) * c:(SCAN_LEVELS + 2) * c]).astype(BF16)
    decay = w[SCAN_EXP_BLOCKS * c:SCAN_EXP_BLOCKS * c + 1]
    qb = qs.astype(BF16)
    kb = kk.astype(BF16)

    t_idx = lax.broadcasted_iota(jnp.int32, (c, c), 0)
    s_idx = lax.broadcasted_iota(jnp.int32, (c, c), 1)
    diff_bits = t_idx ^ s_idx
    causal = ((t_idx - s_idx) * (1 - 2 * d)) >= 0

    n_heads = q_ref.shape[1] // c
    for hd in range(n_heads):
        sl = slice(hd * c, (hd + 1) * c)
        sc = lax.dot_general(qb[:, sl], kb[:, sl], _NT, preferred_element_type=F32)
        for lv in range(SCAN_LEVELS):
            g = lax.dot_general(xs[lv][:, sl], xs[lv][:, sl], _NT, preferred_element_type=F32)
            sc = jnp.where((diff_bits >> lv) == 1, g, sc)
        sc = jnp.where(causal, sc, 0.0)
        st = st_scr[hd]
        o = jnp.dot(sc.astype(BF16), v[:, sl], preferred_element_type=F32)
        o = o + lax.dot_general(q_in[:, sl], st.astype(BF16), _NT, preferred_element_type=F32)
        o_ref[:, sl] = o
        st_scr[hd] = st * decay[:, sl] + lax.dot_general(v[:, sl], k_out[:, sl], _TN,
                                                         preferred_element_type=F32)


def _hgrn_scan(p, lower_bound, layer_idx, batch, seq):
    rows = p.shape[0]
    f = lower_bound.shape[-1]
    c = SCAN_CHUNK
    nc = seq // c
    a_mat = jnp.asarray(_scan_exponent_matrix()).astype(BF16)

    def chunk(b, d, j):
        return b * nc + j + d * (nc - 1 - 2 * j)

    return pl.pallas_call(
        functools.partial(_hgrn_scan_kernel, layer_idx=layer_idx),
        out_shape=jax.ShapeDtypeStruct((2, rows, f), F32),
        grid=(batch, 2, nc),
        in_specs=[
            pl.BlockSpec((c, f), lambda b, d, j: (chunk(b, d, j), 0)),
            pl.BlockSpec((c, f), lambda b, d, j: (chunk(b, d, j), 1 + d)),
            pl.BlockSpec((c, f), lambda b, d, j: (chunk(b, d, j), 3)),
            pl.BlockSpec((None,) + lower_bound.shape[1:], lambda b, d, j: (d, 0, 0)),
            pl.BlockSpec((None,) + a_mat.shape[1:], lambda b, d, j: (d, 0, 0)),
        ],
        out_specs=pl.BlockSpec((None, c, f), lambda b, d, j: (d, chunk(b, d, j), 0)),
        scratch_shapes=[pltpu.VMEM((HGRN_HEADS, c, c), F32)],
        compiler_params=_params("parallel", "parallel", "arbitrary"),
        name="hgrn_scan",
    )(p, p, p, lower_bound, a_mat)


def _hgrn_out_kernel(x_ref, o2_ref, gate_ref, on_ref, wo_ref, g_ref, out_ref):
    o = o2_ref[0] + o2_ref[1]
    gate = _silu(gate_ref[...])
    hd_w = o.shape[1] // HGRN_HEADS
    parts = []
    for hd in range(HGRN_HEADS):
        sl = slice(hd * hd_w, (hd + 1) * hd_w)
        parts.append((_rms(o[:, sl], on_ref[:, sl]) * gate[:, sl]).astype(BF16))
    y = jnp.concatenate(parts, axis=1)
    m = jnp.dot(y, wo_ref[...], preferred_element_type=F32)
    out_ref[...] = x_ref[...] + _rms(m, g_ref[...])


def _hgrn_out(x, o2, p, out_norm, w_o, gain, *, tm=512):
    rows, d = x.shape
    return pl.pallas_call(
        _hgrn_out_kernel,
        out_shape=jax.ShapeDtypeStruct((rows, d), F32),
        grid=(rows // tm,),
        in_specs=[
            pl.BlockSpec((tm, d), lambda i: (i, 0)),
            pl.BlockSpec((2, tm, d), lambda i: (0, i, 0)),
            pl.BlockSpec((tm, d), lambda i: (i, 4)),
            _resident((1, d)),
            _resident(w_o.shape),
            _resident((1, d)),
        ],
        out_specs=pl.BlockSpec((tm, d), lambda i: (i, 0)),
        compiler_params=_params("parallel"),
        name="hgrn_out",
    )(x, o2, p, out_norm.reshape(1, d), w_o.astype(BF16), gain.reshape(1, d))


def _hgrn_layer(x, batch, seq, layer_idx, g_pre, g_post, w_in, lower_bound, out_norm, w_o):
    p = _hgrn_proj(x, g_pre, w_in)
    o2 = _hgrn_scan(p, lower_bound, layer_idx, batch, seq)
    return _hgrn_out(x, o2, p, out_norm, w_o, g_post)


def kernel(x, positions, norm_gains, ffn_w_in, ffn_w_out, mla_w_in, mla_q_norm, mla_kv_norm, mla_w_uq,
           mla_w_ukv, mla_w_o, hgrn_w_in, hgrn_lower_bound, hgrn_out_norm, hgrn_w_o):
    batch, seq, d = x.shape
    depth = norm_gains.shape[0]
    xf = x.reshape(batch * seq, d)
    pos = positions.reshape(batch * seq)
    for l in range(depth):
        g = norm_gains[l]
        xf = _ffn(xf, g[0], g[1], ffn_w_in[l, 0], ffn_w_out[l, 0])
        j = l // N_MIXERS
        if l % N_MIXERS == 0:
            xf = _mla_layer(xf, pos, batch, seq, g[2], g[3], mla_w_in[j], mla_q_norm[j], mla_kv_norm[j],
                            mla_w_uq[j], mla_w_ukv[j], mla_w_o[j])
        else:
            xf = _hgrn_layer(xf, batch, seq, l, g[2], g[3], hgrn_w_in[j], hgrn_lower_bound,
                             hgrn_out_norm[j], hgrn_w_o[j])
        xf = _ffn(xf, g[4], g[5], ffn_w_in[l, 1], ffn_w_out[l, 1])
    return xf.reshape(batch, seq, d)
```

```python
import functools
import math

import numpy as np
import jax
import jax.numpy as jnp
from jax import lax
from jax.experimental import pallas as pl
from jax.experimental.pallas import tpu as pltpu

EPS = 1e-6
ROPE_THETA = 10000.0
MLA_HEADS = 8
QK_NOPE_DIM = 128
QK_ROPE_DIM = 64
V_HEAD_DIM = 128
Q_LORA_RANK = 256
KV_LORA_RANK = 128
HGRN_HEADS = 8
N_MIXERS = 2

LANES = 128
QK_PAD_DIM = 256
SCAN_CHUNK = 128
SCAN_LEVELS = 7
VMEM_LIMIT = 56 * 1024 * 1024

BF16 = jnp.bfloat16
F32 = jnp.float32

_NT = (((1,), (1,)), ((), ()))
_TN = (((0,), (0,)), ((), ()))


def _params(*semantics):
    return pltpu.CompilerParams(dimension_semantics=semantics, vmem_limit_bytes=VMEM_LIMIT)


def _resident(shape):
    nd = len(shape)
    return pl.BlockSpec(shape, lambda *_: (0,) * nd, pipeline_mode=pl.Buffered(1))


def _rms(y, gain):
    return y * lax.rsqrt(jnp.mean(y * y, axis=-1, keepdims=True) + EPS) * gain


def _silu(t):
    return t * (1.0 / (1.0 + jnp.exp(-t)))


FFN_CHUNK = 256


def _ffn_kernel(x_ref, gin_ref, gout_ref, wgu_ref, wout_ref, o_ref, a_scr, *, n_chunks):
    x = x_ref[...]
    h = _rms(x, gin_ref[...]).astype(BF16)
    for c in range(n_chunks):
        gu = jnp.dot(h, wgu_ref[:, c * 2 * FFN_CHUNK:(c + 1) * 2 * FFN_CHUNK],
                     preferred_element_type=F32)
        gate = gu[:, :FFN_CHUNK]
        up = gu[:, FFN_CHUNK:]
        a_scr[:, c * FFN_CHUNK:(c + 1) * FFN_CHUNK] = (_silu(gate) * up).astype(BF16)
    y = jnp.dot(a_scr[...], wout_ref[...], preferred_element_type=F32)
    o_ref[...] = x + 0.5 * _rms(y, gout_ref[...])


def _ffn(x, g_in, g_out, w_in, w_out, *, tm=512):
    rows, d = x.shape
    d_ff = w_out.shape[0]
    n_chunks = d_ff // FFN_CHUNK
    wg = w_in[:, :d_ff].reshape(d, n_chunks, FFN_CHUNK)
    wu = w_in[:, d_ff:].reshape(d, n_chunks, FFN_CHUNK)
    wgu = jnp.concatenate([wg, wu], axis=2).reshape(d, 2 * d_ff).astype(BF16)
    return pl.pallas_call(
        functools.partial(_ffn_kernel, n_chunks=n_chunks),
        out_shape=jax.ShapeDtypeStruct((rows, d), F32),
        grid=(rows // tm,),
        in_specs=[
            pl.BlockSpec((tm, d), lambda i: (i, 0)),
            _resident((1, d)),
            _resident((1, d)),
            _resident((d, 2 * d_ff)),
            _resident((d_ff, d)),
        ],
        out_specs=pl.BlockSpec((tm, d), lambda i: (i, 0)),
        scratch_shapes=[pltpu.VMEM((tm, d_ff), BF16)],
        compiler_params=_params("parallel"),
        name="ffn",
    )(x, g_in.reshape(1, d), g_out.reshape(1, d), wgu, w_out.astype(BF16))


def _mla_out_kernel(x_ref, o_ref_in, wo_ref, g_ref, out_ref):
    m = jnp.dot(o_ref_in[...], wo_ref[...], preferred_element_type=F32)
    out_ref[...] = x_ref[...] + _rms(m, g_ref[...])


def _mla_out(x, o, w_o, gain, *, tm=512):
    rows, d = x.shape
    return pl.pallas_call(
        _mla_out_kernel,
        out_shape=jax.ShapeDtypeStruct((rows, d), F32),
        grid=(rows // tm,),
        in_specs=[
            pl.BlockSpec((tm, d), lambda i: (i, 0)),
            pl.BlockSpec((tm, o.shape[1]), lambda i: (i, 0)),
            _resident(w_o.shape),
            _resident((1, d)),
        ],
        out_specs=pl.BlockSpec((tm, d), lambda i: (i, 0)),
        compiler_params=_params("parallel"),
        name="mla_out",
    )(x, o, w_o.astype(BF16), gain.reshape(1, d))


def _mla_proj_kernel(x_ref, pos_ref, g_ref, win_ref, qn_ref, kvn_ref, wuq_ref, wuqs_ref, wukv_ref,
                     invf_ref, sign_ref, q_ref, k_ref, v_ref, *, q_scale):
    h = _rms(x_ref[...], g_ref[...]).astype(BF16)
    p = jnp.dot(h, win_ref[...], preferred_element_type=F32)
    cq = _rms(p[:, :Q_LORA_RANK], qn_ref[...] * q_scale).astype(BF16)
    ckv = _rms(p[:, Q_LORA_RANK:Q_LORA_RANK + KV_LORA_RANK], kvn_ref[...]).astype(BF16)
    kr = p[:, Q_LORA_RANK + KV_LORA_RANK:Q_LORA_RANK + KV_LORA_RANK + LANES]
    kr_sw = p[:, Q_LORA_RANK + KV_LORA_RANK + LANES:]
    ang = pos_ref[...].astype(F32) * invf_ref[...]
    cos = jnp.cos(ang)
    sin = jnp.sin(ang) * sign_ref[...]
    k_rope = (kr * cos + kr_sw * sin).astype(BF16)
    q_all = jnp.dot(cq, wuq_ref[...], preferred_element_type=F32)
    q_sw = jnp.dot(cq, wuqs_ref[...], preferred_element_type=F32)
    kv = jnp.dot(ckv, wukv_ref[...], preferred_element_type=F32)
    lane = lax.broadcasted_iota(jnp.int32, (kv.shape[0], LANES), 1)
    ones_col = jnp.where(lane == 0, 1.0, 0.0).astype(BF16)
    for hd in range(MLA_HEADS):
        b0 = hd * QK_PAD_DIM
        q_ref[:, b0:b0 + LANES] = q_all[:, b0:b0 + LANES].astype(BF16)
        q_ref[:, b0 + LANES:b0 + 2 * LANES] = (
            q_all[:, b0 + LANES:b0 + 2 * LANES] * cos + q_sw[:, hd * LANES:(hd + 1) * LANES] * sin
        ).astype(BF16)
        k_ref[:, b0:b0 + LANES] = kv[:, b0:b0 + LANES].astype(BF16)
        k_ref[:, b0 + LANES:b0 + 2 * LANES] = k_rope
        v_ref[:, b0:b0 + LANES] = kv[:, b0 + LANES:b0 + 2 * LANES].astype(BF16)
        v_ref[:, b0 + LANES:b0 + 2 * LANES] = ones_col


def _pad_rope_cols(w, swap):
    half = QK_ROPE_DIM // 2
    a, b = w[:, :half], w[:, half:]
    if swap:
        a, b = b, a
    return jnp.concatenate([a, b, jnp.zeros((w.shape[0], LANES - QK_ROPE_DIM), w.dtype)], axis=1)


def _mla_proj(x, pos, gain, w_in, q_norm, kv_norm, w_uq, w_ukv, *, tm=512):
    rows, d = x.shape
    lat = Q_LORA_RANK + KV_LORA_RANK
    w_kr = w_in[:, lat:]
    win_ext = jnp.concatenate(
        [w_in[:, :lat], _pad_rope_cols(w_kr, False), _pad_rope_cols(w_kr, True)], axis=1).astype(BF16)
    wq = w_uq.reshape(Q_LORA_RANK, MLA_HEADS, QK_NOPE_DIM + QK_ROPE_DIM)
    wuq_ext = jnp.concatenate(
        [jnp.concatenate([wq[:, hd, :QK_NOPE_DIM], _pad_rope_cols(wq[:, hd, QK_NOPE_DIM:], False)], axis=1)
         for hd in range(MLA_HEADS)], axis=1).astype(BF16)
    wuq_sw = jnp.concatenate(
        [_pad_rope_cols(wq[:, hd, QK_NOPE_DIM:], True) for hd in range(MLA_HEADS)], axis=1).astype(BF16)
    half = QK_ROPE_DIM // 2
    inv_freq = ROPE_THETA ** (-np.arange(0, QK_ROPE_DIM, 2, dtype=np.float32) / QK_ROPE_DIM)
    invf = np.zeros((1, LANES), np.float32)
    invf[0, :half] = inv_freq
    invf[0, half:2 * half] = inv_freq
    sign = np.zeros((1, LANES), np.float32)
    sign[0, :half] = -1.0
    sign[0, half:2 * half] = 1.0
    q_scale = (QK_NOPE_DIM + QK_ROPE_DIM) ** -0.5 * math.log2(math.e)
    hq = MLA_HEADS * QK_PAD_DIM
    return pl.pallas_call(
        functools.partial(_mla_proj_kernel, q_scale=q_scale),
        out_shape=(jax.ShapeDtypeStruct((rows, hq), BF16),
                   jax.ShapeDtypeStruct((rows, hq), BF16),
                   jax.ShapeDtypeStruct((rows, hq), BF16)),
        grid=(rows // tm,),
        in_specs=[
            pl.BlockSpec((tm, d), lambda i: (i, 0)),
            pl.BlockSpec((tm, 1), lambda i: (i, 0)),
            _resident((1, d)),
            _resident(win_ext.shape),
            _resident((1, Q_LORA_RANK)),
            _resident((1, KV_LORA_RANK)),
            _resident(wuq_ext.shape),
            _resident(wuq_sw.shape),
            _resident(w_ukv.shape),
            _resident((1, LANES)),
            _resident((1, LANES)),
        ],
        out_specs=(pl.BlockSpec((tm, hq), lambda i: (i, 0)),
                   pl.BlockSpec((tm, hq), lambda i: (i, 0)),
                   pl.BlockSpec((tm, hq), lambda i: (i, 0))),
        compiler_params=_params("parallel"),
        name="mla_proj",
    )(x, pos.reshape(rows, 1), gain.reshape(1, d), win_ext, q_norm.reshape(1, -1), kv_norm.reshape(1, -1),
      wuq_ext, wuq_sw, w_ukv.astype(BF16), jnp.asarray(invf), jnp.asarray(sign))


def _attn_kernel(q_ref, k_ref, v_ref, o_ref, *, tk):
    q = q_ref[...]
    tq = q.shape[0]
    n_kv = k_ref.shape[0] // tk

    def scores(j):
        return lax.dot_general(q, k_ref[j * tk:(j + 1) * tk, :], _NT, preferred_element_type=F32)

    m = jnp.full((tq, 1), -jnp.inf, F32)
    acc = jnp.zeros((tq, 2 * V_HEAD_DIM), F32)
    s_next = scores(0)
    for j in range(n_kv):
        s = s_next
        if j + 1 < n_kv:
            s_next = scores(j + 1)
        m_new = jnp.maximum(m, jnp.max(s, axis=-1, keepdims=True))
        alpha = jnp.exp2(m - m_new)
        p = jnp.exp2(s - m_new).astype(BF16)
        acc = alpha * acc + jnp.dot(p, v_ref[j * tk:(j + 1) * tk, :], preferred_element_type=F32)
        m = m_new
    o_ref[...] = (acc[:, :V_HEAD_DIM] / acc[:, V_HEAD_DIM:V_HEAD_DIM + 1]).astype(o_ref.dtype)


def _attention(q, k, v, batch, seq, *, tq=512, tk=1024):
    rows = q.shape[0]
    tq = min(tq, seq)
    tk = min(tk, seq)
    nq = seq // tq
    return pl.pallas_call(
        functools.partial(_attn_kernel, tk=tk),
        out_shape=jax.ShapeDtypeStruct((rows, MLA_HEADS * V_HEAD_DIM), BF16),
        grid=(batch, MLA_HEADS, nq),
        in_specs=[
            pl.BlockSpec((tq, QK_PAD_DIM), lambda b, h, i: (b * nq + i, h)),
            pl.BlockSpec((seq, QK_PAD_DIM), lambda b, h, i: (b, h)),
            pl.BlockSpec((seq, 2 * V_HEAD_DIM), lambda b, h, i: (b, h)),
        ],
        out_specs=pl.BlockSpec((tq, V_HEAD_DIM), lambda b, h, i: (b * nq + i, h)),
        compiler_params=_params("parallel", "parallel", "parallel"),
        name="mla_attn",
    )(q, k, v)


def _mla_layer(x, pos, batch, seq, g_pre, g_post, w_in, q_norm, kv_norm, w_uq, w_ukv, w_o):
    q, k, v = _mla_proj(x, pos, g_pre, w_in, q_norm, kv_norm, w_uq, w_ukv)
    o = _attention(q, k, v, batch, seq)
    return _mla_out(x, o, w_o, g_post)


def _hgrn_proj_kernel(x_ref, g_ref, w_ref, o_ref, h_scr):
    @pl.when(pl.program_id(1) == 0)
    def _():
        h_scr[...] = _rms(x_ref[...], g_ref[...]).astype(BF16)

    o_ref[...] = jnp.dot(h_scr[...], w_ref[...], preferred_element_type=F32)


def _hgrn_proj(x, gain, w_in, *, tm=1024):
    rows, d = x.shape
    n_out = w_in.shape[1]
    return pl.pallas_call(
        _hgrn_proj_kernel,
        out_shape=jax.ShapeDtypeStruct((rows, n_out), F32),
        grid=(rows // tm, n_out // d),
        in_specs=[
            pl.BlockSpec((tm, d), lambda i, j: (i, 0)),
            _resident((1, d)),
            pl.BlockSpec((d, d), lambda i, j: (0, j)),
        ],
        out_specs=pl.BlockSpec((tm, d), lambda i, j: (i, j)),
        scratch_shapes=[pltpu.VMEM((tm, d), BF16)],
        compiler_params=_params("parallel", "arbitrary"),
        name="hgrn_proj",
    )(x, gain.reshape(1, d), w_in.astype(BF16))


SCAN_EXP_BLOCKS = SCAN_LEVELS + 2
SCAN_TOTAL_ROWS = 16


def _scan_exponent_matrix():
    c = SCAN_CHUNK
    out = np.zeros((2, SCAN_EXP_BLOCKS * c + SCAN_TOTAL_ROWS, c), np.float32)
    r = np.arange(c)
    for t in range(c):
        for lv in range(SCAN_LEVELS):
            mid = ((t >> lv) | 1) << lv
            if (t >> lv) & 1:
                out[0, lv * c + t, (r >= mid) & (r <= t)] = 1.0
                out[1, lv * c + t, (r >= mid) & (r < t)] = 1.0
            else:
                out[0, lv * c + t, (r > t) & (r < mid)] = 1.0
                out[1, lv * c + t, (r >= t) & (r < mid)] = 1.0
        out[0, SCAN_LEVELS * c + t, r <= t] = 1.0
        out[1, SCAN_LEVELS * c + t, r >= t] = 1.0
        out[0, (SCAN_LEVELS + 1) * c + t, r > t] = 1.0
        out[1, (SCAN_LEVELS + 1) * c + t, r < t] = 1.0
    out[:, SCAN_EXP_BLOCKS * c:, :] = 1.0
    return np.concatenate([out, out], axis=2)


def _hgrn_scan_kernel(q_ref, z_ref, i_ref, lbp_ref, a_ref, o_ref, st_scr, *, layer_idx):
    d = pl.program_id(1)
    c = SCAN_CHUNK

    @pl.when(pl.program_id(2) == 0)
    def _():
        st_scr[...] = jnp.zeros_like(st_scr)

    lbp = lbp_ref[...]
    e = jnp.exp(lbp - jnp.max(lbp, axis=0, keepdims=True))
    sm = e / jnp.sum(e, axis=0, keepdims=True)
    lb = jnp.sum(sm[1:layer_idx + 1], axis=0, keepdims=True)

    sg = 1.0 / (1.0 + jnp.exp(-z_ref[...]))
    f = lb + (1.0 - lb) * sg
    kk = (1.0 - lb) * (1.0 - sg)
    logf = jnp.log(f)
    hi = logf.astype(BF16)
    lo = (logf - hi.astype(F32)).astype(BF16)
    expo = jnp.dot(a_ref[...], jnp.concatenate([hi, lo], axis=0), preferred_element_type=F32)
    w = jnp.exp(expo)

    qs = _silu(q_ref[...])
    v = i_ref[...].astype(BF16)
    dqk = qs - kk
    row = lax.broadcasted_iota(jnp.int32, (c, 1), 0)
    xs = []
    for lv in range(SCAN_LEVELS):
        is_q = (((row >> lv) & 1) != d).astype(F32)
        xs.append((w[lv * c:(lv + 1) * c] * (kk + is_q * dqk)).astype(BF16))
    q_in = (qs * w[SCAN_LEVELS * c:(SCAN_LEVELS + 1) * c]).astype(BF16)
    k_out = (kk * w[(SCAN_LEVELS + 1) * c:(SCAN_LEVELS + 2) * c]).astype(BF16)
    decay = w[SCAN_EXP_BLOCKS * c:SCAN_EXP_BLOCKS * c + 1]
    qb = qs.astype(BF16)
    kb = kk.astype(BF16)

    t_idx = lax.broadcasted_iota(jnp.int32, (c, c), 0)
    s_idx = lax.broadcasted_iota(jnp.int32, (c, c), 1)
    diff_bits = t_idx ^ s_idx
    causal = ((t_idx - s_idx) * (1 - 2 * d)) >= 0

    n_heads = q_ref.shape[1] // c
    for hd in range(n_heads):
        sl = slice(hd * c, (hd + 1) * c)
        sc = lax.dot_general(qb[:, sl], kb[:, sl], _NT, preferred_element_type=F32)
        for lv in range(SCAN_LEVELS):
            g = lax.dot_general(xs[lv][:, sl], xs[lv][:, sl], _NT, preferred_element_type=F32)
            sc = jnp.where((diff_bits >> lv) == 1, g, sc)
        sc = jnp.where(causal, sc, 0.0)
        st = st_scr[hd]
        o = jnp.dot(sc.astype(BF16), v[:, sl], preferred_element_type=F32)
        o = o + lax.dot_general(q_in[:, sl], st.astype(BF16), _NT, preferred_element_type=F32)
        o_ref[:, sl] = o
        st_scr[hd] = st * decay[:, sl] + lax.dot_general(v[:, sl], k_out[:, sl], _TN,
                                                         preferred_element_type=F32)


def _hgrn_scan(p, lower_bound, layer_idx, batch, seq):
    rows = p.shape[0]
    f = lower_bound.shape[-1]
    c = SCAN_CHUNK
    nc = seq // c
    a_mat = jnp.asarray(_scan_exponent_matrix()).astype(BF16)

    def chunk(b, d, j):
        return b * nc + j + d * (nc - 1 - 2 * j)

    return pl.pallas_call(
        functools.partial(_hgrn_scan_kernel, layer_idx=layer_idx),
        out_shape=jax.ShapeDtypeStruct((2, rows, f), F32),
        grid=(batch, 2, nc),
        in_specs=[
            pl.BlockSpec((c, f), lambda b, d, j: (chunk(b, d, j), 0)),
            pl.BlockSpec((c, f), lambda b, d, j: (chunk(b, d, j), 1 + d)),
            pl.BlockSpec((c, f), lambda b, d, j: (chunk(b, d, j), 3)),
            pl.BlockSpec((None,) + lower_bound.shape[1:], lambda b, d, j: (d, 0, 0)),
            pl.BlockSpec((None,) + a_mat.shape[1:], lambda b, d, j: (d, 0, 0)),
        ],
        out_specs=pl.BlockSpec((None, c, f), lambda b, d, j: (d, chunk(b, d, j), 0)),
        scratch_shapes=[pltpu.VMEM((HGRN_HEADS, c, c), F32)],
        compiler_params=_params("parallel", "parallel", "arbitrary"),
        name="hgrn_scan",
    )(p, p, p, lower_bound, a_mat)


def _hgrn_out_kernel(x_ref, o2_ref, gate_ref, on_ref, wo_ref, g_ref, out_ref):
    o = o2_ref[0] + o2_ref[1]
    gate = _silu(gate_ref[...])
    hd_w = o.shape[1] // HGRN_HEADS
    parts = []
    for hd in range(HGRN_HEADS):
        sl = slice(hd * hd_w, (hd + 1) * hd_w)
        parts.append((_rms(o[:, sl], on_ref[:, sl]) * gate[:, sl]).astype(BF16))
    y = jnp.concatenate(parts, axis=1)
    m = jnp.dot(y, wo_ref[...], preferred_element_type=F32)
    out_ref[...] = x_ref[...] + _rms(m, g_ref[...])


def _hgrn_out(x, o2, p, out_norm, w_o, gain, *, tm=512):
    rows, d = x.shape
    return pl.pallas_call(
        _hgrn_out_kernel,
        out_shape=jax.ShapeDtypeStruct((rows, d), F32),
        grid=(rows // tm,),
        in_specs=[
            pl.BlockSpec((tm, d), lambda i: (i, 0)),
            pl.BlockSpec((2, tm, d), lambda i: (0, i, 0)),
            pl.BlockSpec((tm, d), lambda i: (i, 4)),
            _resident((1, d)),
            _resident(w_o.shape),
            _resident((1, d)),
        ],
        out_specs=pl.BlockSpec((tm, d), lambda i: (i, 0)),
        compiler_params=_params("parallel"),
        name="hgrn_out",
    )(x, o2, p, out_norm.reshape(1, d), w_o.astype(BF16), gain.reshape(1, d))


def _hgrn_layer(x, batch, seq, layer_idx, g_pre, g_post, w_in, lower_bound, out_norm, w_o):
    p = _hgrn_proj(x, g_pre, w_in)
    o2 = _hgrn_scan(p, lower_bound, layer_idx, batch, seq)
    return _hgrn_out(x, o2, p, out_norm, w_o, g_post)


def kernel(x, positions, norm_gains, ffn_w_in, ffn_w_out, mla_w_in, mla_q_norm, mla_kv_norm, mla_w_uq,
           mla_w_ukv, mla_w_o, hgrn_w_in, hgrn_lower_bound, hgrn_out_norm, hgrn_w_o):
    batch, seq, d = x.shape
    depth = norm_gains.shape[0]
    xf = x.reshape(batch * seq, d)
    pos = positions.reshape(batch * seq)
    for l in range(depth):
        g = norm_gains[l]
        xf = _ffn(xf, g[0], g[1], ffn_w_in[l, 0], ffn_w_out[l, 0])
        j = l // N_MIXERS
        if l % N_MIXERS == 0:
            xf = _mla_layer(xf, pos, batch, seq, g[2], g[3], mla_w_in[j], mla_q_norm[j], mla_kv_norm[j],
                            mla_w_uq[j], mla_w_ukv[j], mla_w_o[j])
        else:
            xf = _hgrn_layer(xf, batch, seq, l, g[2], g[3], hgrn_w_in[j], hgrn_lower_bound,
                             hgrn_out_norm[j], hgrn_w_o[j])
        xf = _ffn(xf, g[4], g[5], ffn_w_in[l, 1], ffn_w_out[l, 1])
    return xf.reshape(batch, seq, d)
```

```python
import functools
import math

import numpy as np
import jax
import jax.numpy as jnp
from jax import lax
from jax.experimental import pallas as pl
from jax.experimental.pallas import tpu as pltpu

EPS = 1e-6
ROPE_THETA = 10000.0
MLA_HEADS = 8
QK_NOPE_DIM = 128
QK_ROPE_DIM = 64
V_HEAD_DIM = 128
Q_LORA_RANK = 256
KV_LORA_RANK = 128
HGRN_HEADS = 8
N_MIXERS = 2
N_NORMS_PER_LAYER = 6

LANES = 128
SUBLANES = 8
BF16_ROWS = 16
QK_PAD_DIM = 256
SCAN_CHUNK = 128
SCAN_LEVELS = 7
VMEM_LIMIT = 56 * 1024 * 1024

BF16 = jnp.bfloat16
F32 = jnp.float32


def _params(*semantics):
    return pltpu.CompilerParams(dimension_semantics=semantics, vmem_limit_bytes=VMEM_LIMIT)


def _resident(shape):
    nd = len(shape)
    return pl.BlockSpec(shape, lambda *_: (0,) * nd, pipeline_mode=pl.Buffered(1))


def _stacked(arr, *lead):
    tail = arr.shape[len(lead):]
    index = tuple(lead) + (0,) * len(tail)
    return pl.BlockSpec((None,) * len(lead) + tail, lambda *_: index, pipeline_mode=pl.Buffered(1))


def _rms(y, gain):
    return y * lax.rsqrt(jnp.mean(y * y, axis=-1, keepdims=True) + EPS) * gain


def _silu(t):
    return t * (1.0 / (1.0 + jnp.exp(-t)))


FFN_CHUNK = 256


def _ffn_kernel(x_ref, gin_ref, gout_ref, wgu_ref, wout_ref, o_ref, a_scr, *, n_chunks):
    x = x_ref[...]
    h = _rms(x, gin_ref[...]).astype(BF16)
    for c in range(n_chunks):
        gu = jnp.dot(h, wgu_ref[:, c * 2 * FFN_CHUNK:(c + 1) * 2 * FFN_CHUNK],
                     preferred_element_type=F32)
        gate = gu[:, :FFN_CHUNK]
        up = gu[:, FFN_CHUNK:]
        a_scr[:, c * FFN_CHUNK:(c + 1) * FFN_CHUNK] = (_silu(gate) * up).astype(BF16)
    y = jnp.dot(a_scr[...], wout_ref[...], preferred_element_type=F32)
    o_ref[...] = x + 0.5 * _rms(y, gout_ref[...])


def _ffn_weights(w_in, w_out):
    d_ff = w_out.shape[-2]
    lead = w_in.shape[:-1]
    n_chunks = d_ff // FFN_CHUNK
    wg = w_in[..., :d_ff].reshape(*lead, n_chunks, FFN_CHUNK)
    wu = w_in[..., d_ff:].reshape(*lead, n_chunks, FFN_CHUNK)
    wgu = jnp.concatenate([wg, wu], axis=-1).reshape(*lead, 2 * d_ff).astype(BF16)
    return wgu, w_out.astype(BF16)


def _ffn(x, gains, wgu, wout, layer, which, *, tm=512):
    rows, d = x.shape
    d_ff = wout.shape[-2]
    tile = pl.BlockSpec((tm, d), lambda i: (i, 0))
    return pl.pallas_call(
        functools.partial(_ffn_kernel, n_chunks=d_ff // FFN_CHUNK),
        out_shape=jax.ShapeDtypeStruct((rows, d), F32),
        grid=(rows // tm,),
        in_specs=[tile, _stacked(gains, layer, 4 * which), _stacked(gains, layer, 4 * which + 1),
                  _stacked(wgu, layer, which), _stacked(wout, layer, which)],
        out_specs=tile,
        scratch_shapes=[pltpu.VMEM((tm, d_ff), BF16)],
        compiler_params=_params("parallel"),
        name="ffn",
    )(x, gains, gains, wgu, wout)


def _mla_out_kernel(x_ref, o_ref_in, wo_ref, g_ref, out_ref):
    m = jnp.dot(o_ref_in[...], wo_ref[...], preferred_element_type=F32)
    out_ref[...] = x_ref[...] + _rms(m, g_ref[...])


def _mla_out(x, o, w_o, gains, layer, j, *, tm=512):
    rows, d = x.shape
    tile = pl.BlockSpec((tm, d), lambda i: (i, 0))
    return pl.pallas_call(
        _mla_out_kernel,
        out_shape=jax.ShapeDtypeStruct((rows, d), F32),
        grid=(rows // tm,),
        in_specs=[tile, pl.BlockSpec((tm, o.shape[1]), lambda i: (i, 0)), _stacked(w_o, j),
                  _stacked(gains, layer, 3)],
        out_specs=tile,
        compiler_params=_params("parallel"),
        name="mla_out",
    )(x, o, w_o, gains)


def _mla_proj_kernel(x_ref, pos_ref, g_ref, win_ref, qn_ref, kvn_ref, wuq_ref, wuqs_ref, wukv_ref,
                     invf_ref, sign_ref, q_ref, k_ref, v_ref, *, q_scale):
    h = _rms(x_ref[...], g_ref[...]).astype(BF16)
    p = jnp.dot(h, win_ref[...], preferred_element_type=F32)
    cq = _rms(p[:, :Q_LORA_RANK], qn_ref[...] * q_scale).astype(BF16)
    ckv = _rms(p[:, Q_LORA_RANK:Q_LORA_RANK + KV_LORA_RANK], kvn_ref[...]).astype(BF16)
    kr = p[:, Q_LORA_RANK + KV_LORA_RANK:Q_LORA_RANK + KV_LORA_RANK + LANES]
    kr_sw = p[:, Q_LORA_RANK + KV_LORA_RANK + LANES:]
    ang = pos_ref[...].astype(F32) * invf_ref[...]
    cos = jnp.cos(ang)
    sin = jnp.sin(ang) * sign_ref[...]
    k_rope = (kr * cos + kr_sw * sin).astype(BF16)
    q_all = jnp.dot(cq, wuq_ref[...], preferred_element_type=F32)
    q_sw = jnp.dot(cq, wuqs_ref[...], preferred_element_type=F32)
    kv = jnp.dot(ckv, wukv_ref[...], preferred_element_type=F32)
    lane = lax.broadcasted_iota(jnp.int32, (kv.shape[0], LANES), 1)
    ones_col = jnp.where(lane == 0, 1.0, 0.0).astype(BF16)
    for hd in range(MLA_HEADS):
        b0 = hd * QK_PAD_DIM
        q_ref[:, b0:b0 + LANES] = q_all[:, b0:b0 + LANES].astype(BF16)
        q_ref[:, b0 + LANES:b0 + 2 * LANES] = (
            q_all[:, b0 + LANES:b0 + 2 * LANES] * cos + q_sw[:, hd * LANES:(hd + 1) * LANES] * sin
        ).astype(BF16)
        k_ref[:, b0:b0 + LANES] = kv[:, b0:b0 + LANES].astype(BF16)
        k_ref[:, b0 + LANES:b0 + 2 * LANES] = k_rope
        v_ref[:, b0:b0 + LANES] = kv[:, b0 + LANES:b0 + 2 * LANES].astype(BF16)
        v_ref[:, b0 + LANES:b0 + 2 * LANES] = ones_col


def _pad_rope_cols(w, swap):
    half = QK_ROPE_DIM // 2
    a, b = w[..., :half], w[..., half:]
    if swap:
        a, b = b, a
    return jnp.concatenate([a, b, jnp.zeros(w.shape[:-1] + (LANES - QK_ROPE_DIM,), w.dtype)], axis=-1)


def _mla_weights(w_in, w_uq, w_ukv, w_o):
    n = w_in.shape[0]
    lat = Q_LORA_RANK + KV_LORA_RANK
    w_kr = w_in[..., lat:]
    win_ext = jnp.concatenate(
        [w_in[..., :lat], _pad_rope_cols(w_kr, False), _pad_rope_cols(w_kr, True)], axis=-1).astype(BF16)
    wq = w_uq.reshape(n, Q_LORA_RANK, MLA_HEADS, QK_NOPE_DIM + QK_ROPE_DIM)
    rope = wq[..., QK_NOPE_DIM:]
    wuq_ext = jnp.concatenate([wq[..., :QK_NOPE_DIM], _pad_rope_cols(rope, False)], axis=-1)
    wuq_ext = wuq_ext.reshape(n, Q_LORA_RANK, MLA_HEADS * QK_PAD_DIM).astype(BF16)
    wuq_sw = _pad_rope_cols(rope, True).reshape(n, Q_LORA_RANK, MLA_HEADS * LANES).astype(BF16)
    return win_ext, wuq_ext, wuq_sw, w_ukv.astype(BF16), w_o.astype(BF16)


def _rope_constants():
    half = QK_ROPE_DIM // 2
    inv_freq = ROPE_THETA ** (-np.arange(0, QK_ROPE_DIM, 2, dtype=np.float32) / QK_ROPE_DIM)
    invf = np.zeros((1, LANES), np.float32)
    invf[0, :half] = inv_freq
    invf[0, half:2 * half] = inv_freq
    sign = np.zeros((1, LANES), np.float32)
    sign[0, :half] = -1.0
    sign[0, half:2 * half] = 1.0
    return jnp.asarray(invf), jnp.asarray(sign)


def _mla_proj(x, pos, gains, layer, j, win_ext, q_norm, kv_norm, wuq_ext, wuq_sw, w_ukv, *, tm=512):
    rows, d = x.shape
    invf, sign = _rope_constants()
    q_scale = (QK_NOPE_DIM + QK_ROPE_DIM) ** -0.5 * math.log2(math.e)
    hq = MLA_HEADS * QK_PAD_DIM
    out = jax.ShapeDtypeStruct((rows, hq), BF16)
    out_tile = pl.BlockSpec((tm, hq), lambda i: (i, 0))
    return pl.pallas_call(
        functools.partial(_mla_proj_kernel, q_scale=q_scale),
        out_shape=(out, out, out),
        grid=(rows // tm,),
        in_specs=[
            pl.BlockSpec((tm, d), lambda i: (i, 0)),
            pl.BlockSpec((tm, 1), lambda i: (i, 0)),
            _stacked(gains, layer, 2),
            _stacked(win_ext, j),
            _stacked(q_norm, j),
            _stacked(kv_norm, j),
            _stacked(wuq_ext, j),
            _stacked(wuq_sw, j),
            _stacked(w_ukv, j),
            _resident((1, LANES)),
            _resident((1, LANES)),
        ],
        out_specs=(out_tile, out_tile, out_tile),
        compiler_params=_params("parallel"),
        name="mla_proj",
    )(x, pos, gains, win_ext, q_norm, kv_norm, wuq_ext, wuq_sw, w_ukv, invf, sign)


_NT = (((1,), (1,)), ((), ()))


def _attn_kernel(q_ref, k_ref, v_ref, o_ref, *, tk):
    q = q_ref[...]
    tq = q.shape[0]
    n_kv = k_ref.shape[0] // tk

    def scores(j):
        return lax.dot_general(q, k_ref[j * tk:(j + 1) * tk, :], _NT, preferred_element_type=F32)

    m = jnp.full((tq, 1), -jnp.inf, F32)
    acc = jnp.zeros((tq, 2 * V_HEAD_DIM), F32)
    s_next = scores(0)
    for j in range(n_kv):
        s = s_next
        if j + 1 < n_kv:
            s_next = scores(j + 1)
        m_new = jnp.maximum(m, jnp.max(s, axis=-1, keepdims=True))
        alpha = jnp.exp2(m - m_new)
        p = jnp.exp2(s - m_new).astype(BF16)
        acc = alpha * acc + jnp.dot(p, v_ref[j * tk:(j + 1) * tk, :], preferred_element_type=F32)
        m = m_new
    o_ref[...] = (acc[:, :V_HEAD_DIM] / acc[:, V_HEAD_DIM:V_HEAD_DIM + 1]).astype(o_ref.dtype)


def _attention(q, k, v, batch, seq, *, tq=512, tk=1024):
    rows = q.shape[0]
    tq = min(tq, seq)
    tk = min(tk, seq)
    nq = seq // tq
    return pl.pallas_call(
        functools.partial(_attn_kernel, tk=tk),
        out_shape=jax.ShapeDtypeStruct((rows, MLA_HEADS * V_HEAD_DIM), BF16),
        grid=(batch, MLA_HEADS, nq),
        in_specs=[
            pl.BlockSpec((tq, QK_PAD_DIM), lambda b, h, i: (b * nq + i, h)),
            pl.BlockSpec((seq, QK_PAD_DIM), lambda b, h, i: (b, h)),
            pl.BlockSpec((seq, 2 * V_HEAD_DIM), lambda b, h, i: (b, h)),
        ],
        out_specs=pl.BlockSpec((tq, V_HEAD_DIM), lambda b, h, i: (b * nq + i, h)),
        compiler_params=_params("parallel", "parallel", "parallel"),
        name="mla_attn",
    )(q, k, v)


def _hgrn_proj_kernel(x_ref, g_ref, w_ref, lbp_ref, qs_ref, hi_ref, lo_ref, kk_ref, v_ref, gate_ref, *,
                      layer_idx):
    d = x_ref.shape[1]
    h = _rms(x_ref[...], g_ref[...]).astype(BF16)

    def proj(j):
        return jnp.dot(h, w_ref[:, j * d:(j + 1) * d], preferred_element_type=F32)

    qs_ref[...] = _silu(proj(0)).astype(BF16)
    for direction in range(2):
        lbp = lbp_ref[direction]
        e = jnp.exp(lbp - jnp.max(lbp, axis=0, keepdims=True))
        sm = e / jnp.sum(e, axis=0, keepdims=True)
        lb = sm[1:2]
        for r in range(2, layer_idx + 1):
            lb = lb + sm[r:r + 1]
        sg = 1.0 / (1.0 + jnp.exp(-proj(1 + direction)))
        f = lb + (1.0 - lb) * sg
        logf2 = jnp.log2(f)
        hi = logf2.astype(BF16)
        hi_ref[direction] = hi
        lo_ref[direction] = (logf2 - hi.astype(F32)).astype(BF16)
        kk_ref[direction] = (1.0 - f).astype(BF16)
    v_ref[...] = proj(3).astype(BF16)
    gate_ref[...] = _silu(proj(4)).astype(BF16)


def _hgrn_proj(x, gains, layer, j, w_in, lower_bound, *, tm=512):
    rows, d = x.shape
    one = jax.ShapeDtypeStruct((rows, d), BF16)
    two = jax.ShapeDtypeStruct((2, rows, d), BF16)
    spec1 = pl.BlockSpec((tm, d), lambda i: (i, 0))
    spec2 = pl.BlockSpec((2, tm, d), lambda i: (0, i, 0))
    return pl.pallas_call(
        functools.partial(_hgrn_proj_kernel, layer_idx=layer),
        out_shape=(one, two, two, two, one, one),
        grid=(rows // tm,),
        in_specs=[spec1, _stacked(gains, layer, 2), _stacked(w_in, j), _resident(lower_bound.shape)],
        out_specs=(spec1, spec2, spec2, spec2, spec1, spec1),
        compiler_params=_params("parallel"),
        name="hgrn_proj",
    )(x, gains, w_in, lower_bound)


SCAN_EXP_BLOCKS = SCAN_LEVELS + 2
SCAN_TOTAL_ROWS = BF16_ROWS
SCAN_PIPELINE_DEPTH = 4
SCAN_XT_SLOTS = SCAN_LEVELS + 2


def _scan_exponent_matrix():
    c = SCAN_CHUNK
    out = np.zeros((2, SCAN_EXP_BLOCKS * c + SCAN_TOTAL_ROWS, c), np.float32)
    r = np.arange(c)
    for t in range(c):
        for lv in range(SCAN_LEVELS):
            mid = ((t >> lv) | 1) << lv
            if (t >> lv) & 1:
                out[0, lv * c + t, (r >= mid) & (r <= t)] = 1.0
                out[1, lv * c + t, (r >= mid) & (r < t)] = 1.0
            else:
                out[0, lv * c + t, (r > t) & (r < mid)] = 1.0
                out[1, lv * c + t, (r >= t) & (r < mid)] = 1.0
        out[0, SCAN_LEVELS * c + t, r <= t] = 1.0
        out[1, SCAN_LEVELS * c + t, r >= t] = 1.0
        out[0, (SCAN_LEVELS + 1) * c + t, r > t] = 1.0
        out[1, (SCAN_LEVELS + 1) * c + t, r < t] = 1.0
    out[:, SCAN_EXP_BLOCKS * c:, :] = 1.0
    return np.concatenate([out, out], axis=2)


def _scan_prepare(direction, qs_ref, hi_ref, lo_ref, kk_ref, a_ref, xt_scr):
    c = SCAN_CHUNK
    q_bit = 1 - direction
    hl = jnp.concatenate([hi_ref[...], lo_ref[...]], axis=0)
    w = jnp.exp2(jnp.dot(a_ref[direction], hl, preferred_element_type=F32))
    qs = qs_ref[...].astype(F32)
    kk = kk_ref[...].astype(F32)
    n_feat = qs.shape[1]
    xs = []
    for lv in range(SCAN_LEVELS):
        hs = 1 << lv
        if hs >= SUBLANES:
            base = jnp.concatenate(
                [(qs if ((r // hs) & 1) == q_bit else kk)[r:r + hs] for r in range(0, c, hs)], axis=0)
        else:
            sub = lax.broadcasted_iota(jnp.int32, (1, SUBLANES, n_feat), 1)
            pick_q = ((sub >> lv) & 1) == q_bit
            base = jnp.where(pick_q, qs.reshape(c // SUBLANES, SUBLANES, n_feat),
                             kk.reshape(c // SUBLANES, SUBLANES, n_feat)).reshape(c, n_feat)
        xs.append((w[lv * c:(lv + 1) * c] * base).astype(BF16))
        xt_scr[direction, lv] = xs[-1].T
    q_in = (qs * w[SCAN_LEVELS * c:(SCAN_LEVELS + 1) * c]).astype(BF16)
    k_out = (kk * w[(SCAN_LEVELS + 1) * c:(SCAN_LEVELS + 2) * c]).astype(BF16)
    xt_scr[direction, SCAN_LEVELS] = kk_ref[...].T
    xt_scr[direction, SCAN_LEVELS + 1] = k_out.T
    decay = w[SCAN_EXP_BLOCKS * c:SCAN_EXP_BLOCKS * c + SCAN_TOTAL_ROWS].T[:, 0:1]
    return xs, q_in, decay


def _scan_masks(direction):
    c = SCAN_CHUNK
    q_bit = 1 - direction
    t_idx = lax.broadcasted_iota(jnp.int32, (c, c), 0)
    s_idx = lax.broadcasted_iota(jnp.int32, (c, c), 1)
    diff_bits = t_idx ^ s_idx
    causal = (t_idx > s_idx) if direction == 0 else (t_idx < s_idx)
    masks = {"diag": t_idx == s_idx, "full": {}, "q_rows": {}, "q_masks": {}}
    for lv in range(SCAN_LEVELS):
        hs = 1 << lv
        if hs < BF16_ROWS:
            masks["full"][lv] = ((diff_bits >> lv) == 1) & causal
        else:
            rows = [r for r in range(0, c, hs) if ((r // hs) & 1) == q_bit]
            masks["q_rows"][lv] = rows
            masks["q_masks"][lv] = [(diff_bits[r:r + hs] >> lv) == 1 for r in rows]
    return masks


def _scan_level_products(direction, hd, xs, qb, masks, xt_scr):
    c = SCAN_CHUNK
    sl = slice(hd * c, (hd + 1) * c)
    prods = [jnp.dot(qb[:, sl], xt_scr[direction, SCAN_LEVELS, sl, :], preferred_element_type=F32)]
    for lv in range(SCAN_LEVELS):
        x = xs[lv][:, sl]
        xt = xt_scr[direction, lv, sl, :]
        if lv in masks["full"]:
            prods.append(jnp.dot(x, xt, preferred_element_type=F32))
        else:
            hs = 1 << lv
            xq = jnp.concatenate([x[r:r + hs] for r in masks["q_rows"][lv]], axis=0)
            prods.append(jnp.dot(xq, xt, preferred_element_type=F32))
    return prods


def _scan_head_output(direction, hd, prods, masks, v, q_in, decay, o_ref, st_scr, xt_scr):
    c = SCAN_CHUNK
    sl = slice(hd * c, (hd + 1) * c)
    sc = jnp.where(masks["diag"], prods[0], 0.0)
    for lv in range(SCAN_LEVELS):
        g = prods[lv + 1]
        if lv in masks["full"]:
            sc = jnp.where(masks["full"][lv], g, sc)
        else:
            hs = 1 << lv
            rows = masks["q_rows"][lv]
            parts = []
            for r in range(0, c, hs):
                if r in rows:
                    qi = rows.index(r)
                    parts.append(jnp.where(masks["q_masks"][lv][qi], g[qi * hs:(qi + 1) * hs], sc[r:r + hs]))
                else:
                    parts.append(sc[r:r + hs])
            sc = jnp.concatenate(parts, axis=0)
    st = st_scr[direction, hd]
    o = jnp.dot(sc.astype(BF16), v[:, sl], preferred_element_type=F32)
    o = o + jnp.dot(q_in[:, sl], st.astype(BF16), preferred_element_type=F32)
    o_ref[:, sl] = o.astype(o_ref.dtype)
    st_scr[direction, hd] = st * decay[sl] + jnp.dot(xt_scr[direction, SCAN_LEVELS + 1, sl, :], v[:, sl],
                                                     preferred_element_type=F32)


def _hgrn_scan_kernel(qs_f, hi_f, lo_f, kk_f, v_f, qs_b, hi_b, lo_b, kk_b, v_b, a_ref, o_f, o_b,
                      st_scr, xt_scr):
    @pl.when(pl.program_id(1) == 0)
    def _():
        st_scr[...] = jnp.zeros_like(st_scr)

    refs = ((qs_f, hi_f, lo_f, kk_f, v_f, o_f), (qs_b, hi_b, lo_b, kk_b, v_b, o_b))
    prep = [_scan_prepare(d, refs[d][0], refs[d][1], refs[d][2], refs[d][3], a_ref, xt_scr) for d in range(2)]
    masks = [_scan_masks(d) for d in range(2)]
    n_heads = qs_f.shape[1] // SCAN_CHUNK
    units = [(d, hd) for hd in range(n_heads) for d in range(2)]
    pending = []

    def finish(unit, prods):
        d, hd = unit
        _, q_in, decay = prep[d]
        _scan_head_output(d, hd, prods, masks[d], refs[d][4][...], q_in, decay, refs[d][5], st_scr, xt_scr)

    for unit in units:
        d, hd = unit
        pending.append((unit, _scan_level_products(d, hd, prep[d][0], refs[d][0][...], masks[d], xt_scr)))
        if len(pending) > SCAN_PIPELINE_DEPTH:
            finish(*pending.pop(0))
    while pending:
        finish(*pending.pop(0))


def _hgrn_scan(qs, hi, lo, kk, v, batch, seq):
    rows, f = qs.shape
    c = SCAN_CHUNK
    nc = seq // c
    a_mat = jnp.asarray(_scan_exponent_matrix(), dtype=BF16)

    def fwd(b, j):
        return b * nc + j

    def bwd(b, j):
        return b * nc + nc - 1 - j

    def specs(chunk, direction):
        one = pl.BlockSpec((c, f), lambda b, j: (chunk(b, j), 0))
        two = pl.BlockSpec((None, c, f), lambda b, j: (direction, chunk(b, j), 0))
        return [one, two, two, two, one]

    out = jax.ShapeDtypeStruct((rows, f), BF16)
    return pl.pallas_call(
        _hgrn_scan_kernel,
        out_shape=(out, out),
        grid=(batch, nc),
        in_specs=specs(fwd, 0) + specs(bwd, 1) + [_resident(a_mat.shape)],
        out_specs=(pl.BlockSpec((c, f), lambda b, j: (fwd(b, j), 0)),
                   pl.BlockSpec((c, f), lambda b, j: (bwd(b, j), 0))),
        scratch_shapes=[pltpu.VMEM((2, HGRN_HEADS, c, c), F32),
                        pltpu.VMEM((2, SCAN_XT_SLOTS, f, c), BF16)],
        compiler_params=_params("parallel", "arbitrary"),
        name="hgrn_scan",
    )(qs, hi, lo, kk, v, qs, hi, lo, kk, v, a_mat)


def _hgrn_out_kernel(x_ref, of_ref, ob_ref, gate_ref, on_ref, wo_ref, g_ref, out_ref):
    o = of_ref[...].astype(F32) + ob_ref[...].astype(F32)
    gate = gate_ref[...].astype(F32)
    hd_w = o.shape[1] // HGRN_HEADS
    parts = []
    for hd in range(HGRN_HEADS):
        sl = slice(hd * hd_w, (hd + 1) * hd_w)
        parts.append((_rms(o[:, sl], on_ref[:, sl]) * gate[:, sl]).astype(BF16))
    y = jnp.concatenate(parts, axis=1)
    m = jnp.dot(y, wo_ref[...], preferred_element_type=F32)
    out_ref[...] = x_ref[...] + _rms(m, g_ref[...])


def _hgrn_out(x, o_f, o_b, gate, out_norm, w_o, gains, layer, j, *, tm=512):
    rows, d = x.shape
    tile = pl.BlockSpec((tm, d), lambda i: (i, 0))
    return pl.pallas_call(
        _hgrn_out_kernel,
        out_shape=jax.ShapeDtypeStruct((rows, d), F32),
        grid=(rows // tm,),
        in_specs=[tile, tile, tile, tile, _stacked(out_norm, j), _stacked(w_o, j), _stacked(gains, layer, 3)],
        out_specs=tile,
        compiler_params=_params("parallel"),
        name="hgrn_out",
    )(x, o_f, o_b, gate, out_norm, w_o, gains)


def kernel(x, positions, norm_gains, ffn_w_in, ffn_w_out, mla_w_in, mla_q_norm, mla_kv_norm, mla_w_uq,
           mla_w_ukv, mla_w_o, hgrn_w_in, hgrn_lower_bound, hgrn_out_norm, hgrn_w_o):
    batch, seq, d = x.shape
    depth = norm_gains.shape[0]
    rows = batch * seq
    xf = x.reshape(rows, d)
    pos = positions.reshape(rows, 1)
    gains = norm_gains.reshape(depth, N_NORMS_PER_LAYER, 1, d)
    wgu, wout = _ffn_weights(ffn_w_in, ffn_w_out)
    win_ext, wuq_ext, wuq_sw, wukv, mla_wo = _mla_weights(mla_w_in, mla_w_uq, mla_w_ukv, mla_w_o)
    q_norm = mla_q_norm.reshape(-1, 1, Q_LORA_RANK)
    kv_norm = mla_kv_norm.reshape(-1, 1, KV_LORA_RANK)
    hgrn_win = hgrn_w_in.astype(BF16)
    hgrn_wo = hgrn_w_o.astype(BF16)
    out_norm = hgrn_out_norm.reshape(-1, 1, d)
    for l in range(depth):
        xf = _ffn(xf, gains, wgu, wout, l, 0)
        j = l // N_MIXERS
        if l % N_MIXERS == 0:
            q, k, v = _mla_proj(xf, pos, gains, l, j, win_ext, q_norm, kv_norm, wuq_ext, wuq_sw, wukv)
            o = _attention(q, k, v, batch, seq)
            xf = _mla_out(xf, o, mla_wo, gains, l, j)
        else:
            qs, hi, lo, kk, v, gate = _hgrn_proj(xf, gains, l, j, hgrn_win, hgrn_lower_bound)
            o_f, o_b = _hgrn_scan(qs, hi, lo, kk, v, batch, seq)
            xf = _hgrn_out(xf, o_f, o_b, gate, out_norm, hgrn_wo, gains, l, j)
        xf = _ffn(xf, gains, wgu, wout, l, 1)
    return xf.reshape(batch, seq, d)
```

```python
import functools
import math

import numpy as np
import jax
import jax.numpy as jnp
from jax import lax
from jax.experimental import pallas as pl
from jax.experimental.pallas import tpu as pltpu

EPS = 1e-6
ROPE_THETA = 10000.0
MLA_HEADS = 8
QK_NOPE_DIM = 128
QK_ROPE_DIM = 64
V_HEAD_DIM = 128
Q_LORA_RANK = 256
KV_LORA_RANK = 128
HGRN_HEADS = 8
N_MIXERS = 2
N_NORMS_PER_LAYER = 6

LANES = 128
SUBLANES = 8
BF16_ROWS = 16
QK_PAD_DIM = 256
SCAN_CHUNK = 128
SCAN_LEVELS = 7
VMEM_LIMIT = 56 * 1024 * 1024

BF16 = jnp.bfloat16
F32 = jnp.float32


def _params(*semantics):
    return pltpu.CompilerParams(dimension_semantics=semantics, vmem_limit_bytes=VMEM_LIMIT)


def _resident(shape):
    nd = len(shape)
    return pl.BlockSpec(shape, lambda *_: (0,) * nd, pipeline_mode=pl.Buffered(1))


def _stacked(arr, *lead):
    tail = arr.shape[len(lead):]
    index = tuple(lead) + (0,) * len(tail)
    return pl.BlockSpec((None,) * len(lead) + tail, lambda *_: index, pipeline_mode=pl.Buffered(1))


def _rms(y, gain):
    return y * lax.rsqrt(jnp.mean(y * y, axis=-1, keepdims=True) + EPS) * gain


def _silu(t):
    return t * (1.0 / (1.0 + jnp.exp(-t)))


FFN_CHUNK = 256


def _ffn_body(x, gin_ref, gout_ref, w_ref, wout_ref, o_ref, a_scr):
    d_ff = wout_ref.shape[0]
    h = _rms(x, gin_ref[...]).astype(BF16)
    for lo in range(0, d_ff, FFN_CHUNK):
        gate = jnp.dot(h, w_ref[:, lo:lo + FFN_CHUNK], preferred_element_type=F32)
        up = jnp.dot(h, w_ref[:, d_ff + lo:d_ff + lo + FFN_CHUNK], preferred_element_type=F32)
        a_scr[:, lo:lo + FFN_CHUNK] = (_silu(gate) * up).astype(BF16)
    y = jnp.dot(a_scr[...], wout_ref[...], preferred_element_type=F32)
    o_ref[...] = x + 0.5 * _rms(y, gout_ref[...])


def _ffn_kernel(x_ref, *ffn_refs):
    _ffn_body(x_ref[...], *ffn_refs)


def _mla_out_ffn_kernel(x_ref, attn_ref, wo_ref, g_ref, *ffn_refs):
    m = jnp.dot(attn_ref[...], wo_ref[...], preferred_element_type=F32)
    _ffn_body(x_ref[...] + _rms(m, g_ref[...]), *ffn_refs)


def _hgrn_out_ffn_kernel(x_ref, of_ref, ob_ref, gate_ref, on_ref, wo_ref, g_ref, *ffn_refs):
    o = of_ref[...].astype(F32) + ob_ref[...].astype(F32)
    gate = gate_ref[...].astype(F32)
    hd_w = o.shape[1] // HGRN_HEADS
    parts = []
    for hd in range(HGRN_HEADS):
        sl = slice(hd * hd_w, (hd + 1) * hd_w)
        parts.append((_rms(o[:, sl], on_ref[:, sl]) * gate[:, sl]).astype(BF16))
    m = jnp.dot(jnp.concatenate(parts, axis=1), wo_ref[...], preferred_element_type=F32)
    _ffn_body(x_ref[...] + _rms(m, g_ref[...]), *ffn_refs)


def _ffn_call(kernel_fn, name, x, mixer_args, mixer_specs, gains, w, wout, layer, which, *, tm=512):
    rows, d = x.shape
    tile = pl.BlockSpec((tm, d), lambda i: (i, 0))
    return pl.pallas_call(
        kernel_fn,
        out_shape=jax.ShapeDtypeStruct((rows, d), F32),
        grid=(rows // tm,),
        in_specs=[tile] + mixer_specs + [
            _stacked(gains, layer, 4 * which), _stacked(gains, layer, 4 * which + 1),
            _stacked(w, layer, which), _stacked(wout, layer, which)],
        out_specs=tile,
        scratch_shapes=[pltpu.VMEM((tm, wout.shape[-2]), BF16)],
        compiler_params=_params("parallel"),
        name=name,
    )(x, *mixer_args, gains, gains, w, wout)


def _ffn(x, gains, w, wout, layer, which):
    return _ffn_call(_ffn_kernel, "ffn", x, [], [], gains, w, wout, layer, which)


def _mla_out_ffn(x, attn, w_o, j, gains, w, wout, layer, *, tm=512):
    specs = [pl.BlockSpec((tm, attn.shape[1]), lambda i: (i, 0)), _stacked(w_o, j), _stacked(gains, layer, 3)]
    return _ffn_call(_mla_out_ffn_kernel, "mla_out_ffn", x, [attn, w_o, gains], specs, gains, w, wout,
                     layer, 1, tm=tm)


def _hgrn_out_ffn(x, o_f, o_b, gate, out_norm, w_o, j, gains, w, wout, layer, *, tm=512):
    tile = pl.BlockSpec((tm, x.shape[1]), lambda i: (i, 0))
    specs = [tile, tile, tile, _stacked(out_norm, j), _stacked(w_o, j), _stacked(gains, layer, 3)]
    return _ffn_call(_hgrn_out_ffn_kernel, "hgrn_out_ffn", x, [o_f, o_b, gate, out_norm, w_o, gains], specs,
                     gains, w, wout, layer, 1, tm=tm)


def _mla_proj_kernel(x_ref, pos_ref, g_ref, win_ref, qn_ref, kvn_ref, wuq_ref, wuqs_ref, wukv_ref,
                     invf_ref, sign_ref, q_ref, k_ref, v_ref, *, q_scale):
    h = _rms(x_ref[...], g_ref[...]).astype(BF16)
    p = jnp.dot(h, win_ref[...], preferred_element_type=F32)
    cq = _rms(p[:, :Q_LORA_RANK], qn_ref[...] * q_scale).astype(BF16)
    ckv = _rms(p[:, Q_LORA_RANK:Q_LORA_RANK + KV_LORA_RANK], kvn_ref[...]).astype(BF16)
    kr = p[:, Q_LORA_RANK + KV_LORA_RANK:Q_LORA_RANK + KV_LORA_RANK + LANES]
    kr_sw = p[:, Q_LORA_RANK + KV_LORA_RANK + LANES:]
    ang = pos_ref[...].astype(F32) * invf_ref[...]
    cos = jnp.cos(ang)
    sin = jnp.sin(ang) * sign_ref[...]
    k_rope = (kr * cos + kr_sw * sin).astype(BF16)
    q_all = jnp.dot(cq, wuq_ref[...], preferred_element_type=F32)
    q_sw = jnp.dot(cq, wuqs_ref[...], preferred_element_type=F32)
    kv = jnp.dot(ckv, wukv_ref[...], preferred_element_type=F32)
    lane = lax.broadcasted_iota(jnp.int32, (kv.shape[0], LANES), 1)
    ones_col = jnp.where(lane == 0, 1.0, 0.0).astype(BF16)
    for hd in range(MLA_HEADS):
        b0 = hd * QK_PAD_DIM
        q_ref[:, b0:b0 + LANES] = q_all[:, b0:b0 + LANES].astype(BF16)
        q_ref[:, b0 + LANES:b0 + 2 * LANES] = (
            q_all[:, b0 + LANES:b0 + 2 * LANES] * cos + q_sw[:, hd * LANES:(hd + 1) * LANES] * sin
        ).astype(BF16)
        k_ref[:, b0:b0 + LANES] = kv[:, b0:b0 + LANES].astype(BF16)
        k_ref[:, b0 + LANES:b0 + 2 * LANES] = k_rope
        v_ref[:, b0:b0 + LANES] = kv[:, b0 + LANES:b0 + 2 * LANES].astype(BF16)
        v_ref[:, b0 + LANES:b0 + 2 * LANES] = ones_col


def _pad_rope_cols(w, swap):
    half = QK_ROPE_DIM // 2
    a, b = w[..., :half], w[..., half:]
    if swap:
        a, b = b, a
    return jnp.concatenate([a, b, jnp.zeros(w.shape[:-1] + (LANES - QK_ROPE_DIM,), w.dtype)], axis=-1)


def _mla_weights(w_in, w_uq, w_ukv, w_o):
    n = w_in.shape[0]
    lat = Q_LORA_RANK + KV_LORA_RANK
    w_kr = w_in[..., lat:]
    win_ext = jnp.concatenate(
        [w_in[..., :lat], _pad_rope_cols(w_kr, False), _pad_rope_cols(w_kr, True)], axis=-1).astype(BF16)
    wq = w_uq.reshape(n, Q_LORA_RANK, MLA_HEADS, QK_NOPE_DIM + QK_ROPE_DIM)
    rope = wq[..., QK_NOPE_DIM:]
    wuq_ext = jnp.concatenate([wq[..., :QK_NOPE_DIM], _pad_rope_cols(rope, False)], axis=-1)
    wuq_ext = wuq_ext.reshape(n, Q_LORA_RANK, MLA_HEADS * QK_PAD_DIM).astype(BF16)
    wuq_sw = _pad_rope_cols(rope, True).reshape(n, Q_LORA_RANK, MLA_HEADS * LANES).astype(BF16)
    return win_ext, wuq_ext, wuq_sw, w_ukv.astype(BF16), w_o.astype(BF16)


def _rope_constants():
    half = QK_ROPE_DIM // 2
    inv_freq = ROPE_THETA ** (-np.arange(0, QK_ROPE_DIM, 2, dtype=np.float32) / QK_ROPE_DIM)
    invf = np.zeros((1, LANES), np.float32)
    invf[0, :half] = inv_freq
    invf[0, half:2 * half] = inv_freq
    sign = np.zeros((1, LANES), np.float32)
    sign[0, :half] = -1.0
    sign[0, half:2 * half] = 1.0
    return jnp.asarray(invf), jnp.asarray(sign)


def _mla_proj(x, pos, gains, layer, j, win_ext, q_norm, kv_norm, wuq_ext, wuq_sw, w_ukv, *, tm=512):
    rows, d = x.shape
    invf, sign = _rope_constants()
    q_scale = (QK_NOPE_DIM + QK_ROPE_DIM) ** -0.5 * math.log2(math.e)
    hq = MLA_HEADS * QK_PAD_DIM
    out = jax.ShapeDtypeStruct((rows, hq), BF16)
    out_tile = pl.BlockSpec((tm, hq), lambda i: (i, 0))
    return pl.pallas_call(
        functools.partial(_mla_proj_kernel, q_scale=q_scale),
        out_shape=(out, out, out),
        grid=(rows // tm,),
        in_specs=[
            pl.BlockSpec((tm, d), lambda i: (i, 0)),
            pl.BlockSpec((tm, 1), lambda i: (i, 0)),
            _stacked(gains, layer, 2),
            _stacked(win_ext, j),
            _stacked(q_norm, j),
            _stacked(kv_norm, j),
            _stacked(wuq_ext, j),
            _stacked(wuq_sw, j),
            _stacked(w_ukv, j),
            _resident((1, LANES)),
            _resident((1, LANES)),
        ],
        out_specs=(out_tile, out_tile, out_tile),
        compiler_params=_params("parallel"),
        name="mla_proj",
    )(x, pos, gains, win_ext, q_norm, kv_norm, wuq_ext, wuq_sw, w_ukv, invf, sign)


_NT = (((1,), (1,)), ((), ()))


ATTN_SUB_TILES = 2


def _attn_kernel(q_ref, k_ref, v_ref, o_ref, *, tk):
    tq = q_ref.shape[0] // ATTN_SUB_TILES
    n_kv = k_ref.shape[0] // tk
    for sub in range(ATTN_SUB_TILES):
        rows = slice(sub * tq, (sub + 1) * tq)
        q = q_ref[rows, :]

        def scores(j, q=q):
            return lax.dot_general(q, k_ref[j * tk:(j + 1) * tk, :], _NT, preferred_element_type=F32)

        m = jnp.full((tq, 1), -jnp.inf, F32)
        acc = jnp.zeros((tq, 2 * V_HEAD_DIM), F32)
        s_next = scores(0)
        for j in range(n_kv):
            s = s_next
            if j + 1 < n_kv:
                s_next = scores(j + 1)
            m_new = jnp.maximum(m, jnp.max(s, axis=-1, keepdims=True))
            alpha = jnp.exp2(m - m_new)
            p = jnp.exp2(s - m_new).astype(BF16)
            acc = alpha * acc + jnp.dot(p, v_ref[j * tk:(j + 1) * tk, :], preferred_element_type=F32)
            m = m_new
        o_ref[rows, :] = (acc[:, :V_HEAD_DIM] / acc[:, V_HEAD_DIM:V_HEAD_DIM + 1]).astype(o_ref.dtype)


def _attention(q, k, v, batch, seq, *, tq=1024, tk=1024):
    rows = q.shape[0]
    tq = min(tq, seq)
    tk = min(tk, seq)
    nq = seq // tq
    return pl.pallas_call(
        functools.partial(_attn_kernel, tk=tk),
        out_shape=jax.ShapeDtypeStruct((rows, MLA_HEADS * V_HEAD_DIM), BF16),
        grid=(batch, MLA_HEADS, nq),
        in_specs=[
            pl.BlockSpec((tq, QK_PAD_DIM), lambda b, h, i: (b * nq + i, h)),
            pl.BlockSpec((seq, QK_PAD_DIM), lambda b, h, i: (b, h)),
            pl.BlockSpec((seq, 2 * V_HEAD_DIM), lambda b, h, i: (b, h)),
        ],
        out_specs=pl.BlockSpec((tq, V_HEAD_DIM), lambda b, h, i: (b * nq + i, h)),
        compiler_params=_params("parallel", "parallel", "parallel"),
        name="mla_attn",
    )(q, k, v)


def _hgrn_proj_kernel(x_ref, g_ref, w_ref, lbp_ref, qs_ref, hi_ref, lo_ref, kk_ref, v_ref, gate_ref, *,
                      layer_idx):
    d = x_ref.shape[1]
    h = _rms(x_ref[...], g_ref[...]).astype(BF16)

    def proj(j):
        return jnp.dot(h, w_ref[:, j * d:(j + 1) * d], preferred_element_type=F32)

    qs_ref[...] = _silu(proj(0)).astype(BF16)
    for direction in range(2):
        lbp = lbp_ref[direction]
        e = jnp.exp(lbp - jnp.max(lbp, axis=0, keepdims=True))
        sm = e / jnp.sum(e, axis=0, keepdims=True)
        lb = sm[1:2]
        for r in range(2, layer_idx + 1):
            lb = lb + sm[r:r + 1]
        sg = 1.0 / (1.0 + jnp.exp(-proj(1 + direction)))
        f = lb + (1.0 - lb) * sg
        logf2 = jnp.log2(f)
        hi = logf2.astype(BF16)
        hi_ref[direction] = hi
        lo_ref[direction] = (logf2 - hi.astype(F32)).astype(BF16)
        kk_ref[direction] = (1.0 - f).astype(BF16)
    v_ref[...] = proj(3).astype(BF16)
    gate_ref[...] = _silu(proj(4)).astype(BF16)


def _hgrn_proj(x, gains, layer, j, w_in, lower_bound, *, tm=512):
    rows, d = x.shape
    one = jax.ShapeDtypeStruct((rows, d), BF16)
    two = jax.ShapeDtypeStruct((2, rows, d), BF16)
    spec1 = pl.BlockSpec((tm, d), lambda i: (i, 0))
    spec2 = pl.BlockSpec((2, tm, d), lambda i: (0, i, 0))
    return pl.pallas_call(
        functools.partial(_hgrn_proj_kernel, layer_idx=layer),
        out_shape=(one, two, two, two, one, one),
        grid=(rows // tm,),
        in_specs=[spec1, _stacked(gains, layer, 2), _stacked(w_in, j), _resident(lower_bound.shape)],
        out_specs=(spec1, spec2, spec2, spec2, spec1, spec1),
        compiler_params=_params("parallel"),
        name="hgrn_proj",
    )(x, gains, w_in, lower_bound)


SCAN_EXP_BLOCKS = SCAN_LEVELS + 2
SCAN_TOTAL_ROWS = BF16_ROWS
SCAN_PIPELINE_DEPTH = 4
SCAN_XT_SLOTS = SCAN_LEVELS + 2


def _scan_exponent_matrix():
    c = SCAN_CHUNK
    out = np.zeros((2, SCAN_EXP_BLOCKS * c + SCAN_TOTAL_ROWS, c), np.float32)
    r = np.arange(c)
    for t in range(c):
        for lv in range(SCAN_LEVELS):
            mid = ((t >> lv) | 1) << lv
            if (t >> lv) & 1:
                out[0, lv * c + t, (r >= mid) & (r <= t)] = 1.0
                out[1, lv * c + t, (r >= mid) & (r < t)] = 1.0
            else:
                out[0, lv * c + t, (r > t) & (r < mid)] = 1.0
                out[1, lv * c + t, (r >= t) & (r < mid)] = 1.0
        out[0, SCAN_LEVELS * c + t, r <= t] = 1.0
        out[1, SCAN_LEVELS * c + t, r >= t] = 1.0
        out[0, (SCAN_LEVELS + 1) * c + t, r > t] = 1.0
        out[1, (SCAN_LEVELS + 1) * c + t, r < t] = 1.0
    out[:, SCAN_EXP_BLOCKS * c:, :] = 1.0
    return np.concatenate([out, out], axis=2)


def _scan_prepare(direction, qs_ref, hi_ref, lo_ref, kk_ref, a_ref, xt_scr):
    c = SCAN_CHUNK
    q_bit = 1 - direction
    hl = jnp.concatenate([hi_ref[...], lo_ref[...]], axis=0)
    w = jnp.exp2(jnp.dot(a_ref[direction], hl, preferred_element_type=F32))
    qs = qs_ref[...].astype(F32)
    kk = kk_ref[...].astype(F32)
    n_feat = qs.shape[1]
    xs = []
    for lv in range(SCAN_LEVELS):
        hs = 1 << lv
        if hs >= SUBLANES:
            base = jnp.concatenate(
                [(qs if ((r // hs) & 1) == q_bit else kk)[r:r + hs] for r in range(0, c, hs)], axis=0)
        else:
            sub = lax.broadcasted_iota(jnp.int32, (1, SUBLANES, n_feat), 1)
            pick_q = ((sub >> lv) & 1) == q_bit
            base = jnp.where(pick_q, qs.reshape(c // SUBLANES, SUBLANES, n_feat),
                             kk.reshape(c // SUBLANES, SUBLANES, n_feat)).reshape(c, n_feat)
        xs.append((w[lv * c:(lv + 1) * c] * base).astype(BF16))
        xt_scr[direction, lv] = xs[-1].T
    q_in = (qs * w[SCAN_LEVELS * c:(SCAN_LEVELS + 1) * c]).astype(BF16)
    k_out = (kk * w[(SCAN_LEVELS + 1) * c:(SCAN_LEVELS + 2) * c]).astype(BF16)
    xt_scr[direction, SCAN_LEVELS] = kk_ref[...].T
    xt_scr[direction, SCAN_LEVELS + 1] = k_out.T
    decay = w[SCAN_EXP_BLOCKS * c:SCAN_EXP_BLOCKS * c + SCAN_TOTAL_ROWS].T[:, 0:1]
    return xs, q_in, decay


def _scan_masks(direction):
    c = SCAN_CHUNK
    q_bit = 1 - direction
    t_idx = lax.broadcasted_iota(jnp.int32, (c, c), 0)
    s_idx = lax.broadcasted_iota(jnp.int32, (c, c), 1)
    diff_bits = t_idx ^ s_idx
    causal = (t_idx > s_idx) if direction == 0 else (t_idx < s_idx)
    masks = {"diag": t_idx == s_idx, "full": {}, "q_rows": {}, "q_masks": {}}
    for lv in range(SCAN_LEVELS):
        hs = 1 << lv
        if hs < BF16_ROWS:
            masks["full"][lv] = ((diff_bits >> lv) == 1) & causal
        else:
            rows = [r for r in range(0, c, hs) if ((r // hs) & 1) == q_bit]
            masks["q_rows"][lv] = rows
            masks["q_masks"][lv] = [(diff_bits[r:r + hs] >> lv) == 1 for r in rows]
    return masks


def _scan_level_products(direction, hd, xs, qb, masks, xt_scr):
    c = SCAN_CHUNK
    sl = slice(hd * c, (hd + 1) * c)
    prods = [jnp.dot(qb[:, sl], xt_scr[direction, SCAN_LEVELS, sl, :], preferred_element_type=F32)]
    for lv in range(SCAN_LEVELS):
        x = xs[lv][:, sl]
        xt = xt_scr[direction, lv, sl, :]
        if lv in masks["full"]:
            prods.append(jnp.dot(x, xt, preferred_element_type=F32))
        else:
            hs = 1 << lv
            xq = jnp.concatenate([x[r:r + hs] for r in masks["q_rows"][lv]], axis=0)
            prods.append(jnp.dot(xq, xt, preferred_element_type=F32))
    return prods


def _scan_head_output(direction, hd, prods, masks, v, q_in, decay, o_ref, st_scr, xt_scr):
    c = SCAN_CHUNK
    sl = slice(hd * c, (hd + 1) * c)
    sc = jnp.where(masks["diag"], prods[0], 0.0)
    for lv in range(SCAN_LEVELS):
        g = prods[lv + 1]
        if lv in masks["full"]:
            sc = jnp.where(masks["full"][lv], g, sc)
        else:
            hs = 1 << lv
            rows = masks["q_rows"][lv]
            parts = []
            for r in range(0, c, hs):
                if r in rows:
                    qi = rows.index(r)
                    parts.append(jnp.where(masks["q_masks"][lv][qi], g[qi * hs:(qi + 1) * hs], sc[r:r + hs]))
                else:
                    parts.append(sc[r:r + hs])
            sc = jnp.concatenate(parts, axis=0)
    st = st_scr[direction, hd]
    o = jnp.dot(sc.astype(BF16), v[:, sl], preferred_element_type=F32)
    o = o + jnp.dot(q_in[:, sl], st.astype(BF16), preferred_element_type=F32)
    o_ref[:, sl] = o.astype(o_ref.dtype)
    st_scr[direction, hd] = st * decay[sl] + jnp.dot(xt_scr[direction, SCAN_LEVELS + 1, sl, :], v[:, sl],
                                                     preferred_element_type=F32)


def _hgrn_scan_kernel(qs_f, hi_f, lo_f, kk_f, v_f, qs_b, hi_b, lo_b, kk_b, v_b, a_ref, o_f, o_b,
                      st_scr, xt_scr):
    @pl.when(pl.program_id(1) == 0)
    def _():
        st_scr[...] = jnp.zeros_like(st_scr)

    refs = ((qs_f, hi_f, lo_f, kk_f, v_f, o_f), (qs_b, hi_b, lo_b, kk_b, v_b, o_b))
    prep = [_scan_prepare(d, refs[d][0], refs[d][1], refs[d][2], refs[d][3], a_ref, xt_scr) for d in range(2)]
    masks = [_scan_masks(d) for d in range(2)]
    n_heads = qs_f.shape[1] // SCAN_CHUNK
    units = [(d, hd) for hd in range(n_heads) for d in range(2)]
    pending = []

    def finish(unit, prods):
        d, hd = unit
        _, q_in, decay = prep[d]
        _scan_head_output(d, hd, prods, masks[d], refs[d][4][...], q_in, decay, refs[d][5], st_scr, xt_scr)

    for unit in units:
        d, hd = unit
        pending.append((unit, _scan_level_products(d, hd, prep[d][0], refs[d][0][...], masks[d], xt_scr)))
        if len(pending) > SCAN_PIPELINE_DEPTH:
            finish(*pending.pop(0))
    while pending:
        finish(*pending.pop(0))


def _hgrn_scan(qs, hi, lo, kk, v, batch, seq):
    rows, f = qs.shape
    c = SCAN_CHUNK
    nc = seq // c
    a_mat = jnp.asarray(_scan_exponent_matrix(), dtype=BF16)

    def fwd(b, j):
        return b * nc + j

    def bwd(b, j):
        return b * nc + nc - 1 - j

    def specs(chunk, direction):
        one = pl.BlockSpec((c, f), lambda b, j: (chunk(b, j), 0))
        two = pl.BlockSpec((None, c, f), lambda b, j: (direction, chunk(b, j), 0))
        return [one, two, two, two, one]

    out = jax.ShapeDtypeStruct((rows, f), BF16)
    return pl.pallas_call(
        _hgrn_scan_kernel,
        out_shape=(out, out),
        grid=(batch, nc),
        in_specs=specs(fwd, 0) + specs(bwd, 1) + [_resident(a_mat.shape)],
        out_specs=(pl.BlockSpec((c, f), lambda b, j: (fwd(b, j), 0)),
                   pl.BlockSpec((c, f), lambda b, j: (bwd(b, j), 0))),
        scratch_shapes=[pltpu.VMEM((2, HGRN_HEADS, c, c), F32),
                        pltpu.VMEM((2, SCAN_XT_SLOTS, f, c), BF16)],
        compiler_params=_params("parallel", "arbitrary"),
        name="hgrn_scan",
    )(qs, hi, lo, kk, v, qs, hi, lo, kk, v, a_mat)


def kernel(x, positions, norm_gains, ffn_w_in, ffn_w_out, mla_w_in, mla_q_norm, mla_kv_norm, mla_w_uq,
           mla_w_ukv, mla_w_o, hgrn_w_in, hgrn_lower_bound, hgrn_out_norm, hgrn_w_o):
    batch, seq, d = x.shape
    depth = norm_gains.shape[0]
    rows = batch * seq
    xf = x.reshape(rows, d)
    pos = positions.reshape(rows, 1)
    gains = norm_gains.reshape(depth, N_NORMS_PER_LAYER, 1, d)
    ffn_w = ffn_w_in.astype(BF16)
    wout = ffn_w_out.astype(BF16)
    win_ext, wuq_ext, wuq_sw, wukv, mla_wo = _mla_weights(mla_w_in, mla_w_uq, mla_w_ukv, mla_w_o)
    q_norm = mla_q_norm.reshape(-1, 1, Q_LORA_RANK)
    kv_norm = mla_kv_norm.reshape(-1, 1, KV_LORA_RANK)
    hgrn_win = hgrn_w_in.astype(BF16)
    hgrn_wo = hgrn_w_o.astype(BF16)
    out_norm = hgrn_out_norm.reshape(-1, 1, d)
    for l in range(depth):
        xf = _ffn(xf, gains, ffn_w, wout, l, 0)
        j = l // N_MIXERS
        if l % N_MIXERS == 0:
            q, k, v = _mla_proj(xf, pos, gains, l, j, win_ext, q_norm, kv_norm, wuq_ext, wuq_sw, wukv)
            o = _attention(q, k, v, batch, seq)
            xf = _mla_out_ffn(xf, o, mla_wo, j, gains, ffn_w, wout, l)
        else:
            qs, hi, lo, kk, v, gate = _hgrn_proj(xf, gains, l, j, hgrn_win, hgrn_lower_bound)
            o_f, o_b = _hgrn_scan(qs, hi, lo, kk, v, batch, seq)
            xf = _hgrn_out_ffn(xf, o_f, o_b, gate, out_norm, hgrn_wo, j, gains, ffn_w, wout, l)
    return xf.reshape(batch, seq, d)
```

```python
import functools
import math

import numpy as np
import jax
import jax.numpy as jnp
from jax import lax
from jax.experimental import pallas as pl
from jax.experimental.pallas import tpu as pltpu

EPS = 1e-6
ROPE_THETA = 10000.0
MLA_HEADS = 8
QK_NOPE_DIM = 128
QK_ROPE_DIM = 64
V_HEAD_DIM = 128
Q_LORA_RANK = 256
KV_LORA_RANK = 128
HGRN_HEADS = 8
N_MIXERS = 2
N_NORMS_PER_LAYER = 6

LANES = 128
SUBLANES = 8
BF16_ROWS = 16
QK_PAD_DIM = 256
SCAN_CHUNK = 128
SCAN_LEVELS = 7
VMEM_LIMIT = 56 * 1024 * 1024

BF16 = jnp.bfloat16
F32 = jnp.float32


def _params(*semantics):
    return pltpu.CompilerParams(dimension_semantics=semantics, vmem_limit_bytes=VMEM_LIMIT)


def _resident(shape):
    nd = len(shape)
    return pl.BlockSpec(shape, lambda *_: (0,) * nd, pipeline_mode=pl.Buffered(1))


def _stacked(arr, *lead):
    tail = arr.shape[len(lead):]
    index = tuple(lead) + (0,) * len(tail)
    return pl.BlockSpec((None,) * len(lead) + tail, lambda *_: index, pipeline_mode=pl.Buffered(1))


def _rms(y, gain):
    return y * lax.rsqrt(jnp.mean(y * y, axis=-1, keepdims=True) + EPS) * gain


def _silu(t):
    return t * (1.0 / (1.0 + jnp.exp(-t)))


FFN_CHUNK = 256
FFN_SUB_TILES = 2


def _sub_rows(ref):
    sub = ref.shape[0] // FFN_SUB_TILES
    return [slice(s * sub, (s + 1) * sub) for s in range(FFN_SUB_TILES)]


def _ffn_body(xs, rows, gin_ref, gout_ref, w_ref, wout_ref, o_ref, a_scr):
    d_ff = wout_ref.shape[0]
    hs = [_rms(x, gin_ref[...]).astype(BF16) for x in xs]
    for lo in range(0, d_ff, FFN_CHUNK):
        for r, h in zip(rows, hs):
            gate = jnp.dot(h, w_ref[:, lo:lo + FFN_CHUNK], preferred_element_type=F32)
            up = jnp.dot(h, w_ref[:, d_ff + lo:d_ff + lo + FFN_CHUNK], preferred_element_type=F32)
            a_scr[r, lo:lo + FFN_CHUNK] = (_silu(gate) * up).astype(BF16)
    for r, x in zip(rows, xs):
        y = jnp.dot(a_scr[r, :], wout_ref[...], preferred_element_type=F32)
        o_ref[r, :] = x + 0.5 * _rms(y, gout_ref[...])


def _ffn_kernel(x_ref, *ffn_refs):
    rows = _sub_rows(x_ref)
    _ffn_body([x_ref[r, :] for r in rows], rows, *ffn_refs)


def _mla_out_ffn_kernel(x_ref, attn_ref, wo_ref, g_ref, *ffn_refs):
    rows = _sub_rows(x_ref)
    xs = []
    for r in rows:
        m = jnp.dot(attn_ref[r, :], wo_ref[...], preferred_element_type=F32)
        xs.append(x_ref[r, :] + _rms(m, g_ref[...]))
    _ffn_body(xs, rows, *ffn_refs)


def _hgrn_out_ffn_kernel(x_ref, of_ref, ob_ref, gate_ref, on_ref, wo_ref, g_ref, *ffn_refs):
    rows = _sub_rows(x_ref)
    hd_w = x_ref.shape[1] // HGRN_HEADS
    xs = []
    for r in rows:
        o = of_ref[r, :].astype(F32) + ob_ref[r, :].astype(F32)
        gate = gate_ref[r, :].astype(F32)
        parts = []
        for hd in range(HGRN_HEADS):
            sl = slice(hd * hd_w, (hd + 1) * hd_w)
            parts.append((_rms(o[:, sl], on_ref[:, sl]) * gate[:, sl]).astype(BF16))
        m = jnp.dot(jnp.concatenate(parts, axis=1), wo_ref[...], preferred_element_type=F32)
        xs.append(x_ref[r, :] + _rms(m, g_ref[...]))
    _ffn_body(xs, rows, *ffn_refs)


def _ffn_call(kernel_fn, name, x, mixer_args, mixer_specs, gains, w, wout, layer, which, *, tm=512):
    rows, d = x.shape
    tile = pl.BlockSpec((tm, d), lambda i: (i, 0))
    return pl.pallas_call(
        kernel_fn,
        out_shape=jax.ShapeDtypeStruct((rows, d), F32),
        grid=(rows // tm,),
        in_specs=[tile] + mixer_specs + [
            _stacked(gains, layer, 4 * which), _stacked(gains, layer, 4 * which + 1),
            _stacked(w, layer, which), _stacked(wout, layer, which)],
        out_specs=tile,
        scratch_shapes=[pltpu.VMEM((tm, wout.shape[-2]), BF16)],
        compiler_params=_params("parallel"),
        name=name,
    )(x, *mixer_args, gains, gains, w, wout)


def _ffn(x, gains, w, wout, layer, which):
    return _ffn_call(_ffn_kernel, "ffn", x, [], [], gains, w, wout, layer, which)


def _mla_out_ffn(x, attn, w_o, j, gains, w, wout, layer, *, tm=512):
    specs = [pl.BlockSpec((tm, attn.shape[1]), lambda i: (i, 0)), _stacked(w_o, j), _stacked(gains, layer, 3)]
    return _ffn_call(_mla_out_ffn_kernel, "mla_out_ffn", x, [attn, w_o, gains], specs, gains, w, wout,
                     layer, 1, tm=tm)


def _hgrn_out_ffn(x, o_f, o_b, gate, out_norm, w_o, j, gains, w, wout, layer, *, tm=512):
    tile = pl.BlockSpec((tm, x.shape[1]), lambda i: (i, 0))
    specs = [tile, tile, tile, _stacked(out_norm, j), _stacked(w_o, j), _stacked(gains, layer, 3)]
    return _ffn_call(_hgrn_out_ffn_kernel, "hgrn_out_ffn", x, [o_f, o_b, gate, out_norm, w_o, gains], specs,
                     gains, w, wout, layer, 1, tm=tm)


def _rope_kernel(pos_ref, invf_ref, cos_ref, sin_ref):
    ang = pos_ref[...].astype(F32) * invf_ref[...]
    cos_ref[...] = jnp.cos(ang)
    sin_ref[...] = jnp.sin(ang)


def _rope_tables(positions):
    rows = positions.size
    half = QK_ROPE_DIM // 2
    per_row = LANES // half
    inv_freq = ROPE_THETA ** (-np.arange(0, QK_ROPE_DIM, 2, dtype=np.float32) / QK_ROPE_DIM)
    invf = jnp.asarray(np.tile(inv_freq, per_row)[None, :])
    pos = jnp.repeat(positions.reshape(rows // per_row, per_row), half, axis=1)
    tm = min(1024, rows // per_row)
    tile = pl.BlockSpec((tm, LANES), lambda i: (i, 0))
    table = jax.ShapeDtypeStruct((rows // per_row, LANES), F32)
    cos, sin = pl.pallas_call(
        _rope_kernel,
        out_shape=(table, table),
        grid=(rows // per_row // tm,),
        in_specs=[tile, _resident((1, LANES))],
        out_specs=(tile, tile),
        compiler_params=_params("parallel"),
        name="rope_tables",
    )(pos, invf)

    def per_token(t):
        t = t.reshape(rows, half)
        return jnp.concatenate([t, t, jnp.zeros((rows, LANES - QK_ROPE_DIM), F32)], axis=1)

    return per_token(cos), per_token(sin)


def _mla_proj_kernel(x_ref, cos_ref, sin_ref, g_ref, win_ref, qn_ref, kvn_ref, wuq_ref, wuqs_ref, wukv_ref,
                     sign_ref, q_ref, k_ref, v_ref, *, q_scale):
    h = _rms(x_ref[...], g_ref[...]).astype(BF16)
    p = jnp.dot(h, win_ref[...], preferred_element_type=F32)
    cq = _rms(p[:, :Q_LORA_RANK], qn_ref[...] * q_scale).astype(BF16)
    ckv = _rms(p[:, Q_LORA_RANK:Q_LORA_RANK + KV_LORA_RANK], kvn_ref[...]).astype(BF16)
    kr = p[:, Q_LORA_RANK + KV_LORA_RANK:Q_LORA_RANK + KV_LORA_RANK + LANES]
    kr_sw = p[:, Q_LORA_RANK + KV_LORA_RANK + LANES:]
    cos = cos_ref[...]
    sin = sin_ref[...] * sign_ref[...]
    k_rope = (kr * cos + kr_sw * sin).astype(BF16)
    q_all = jnp.dot(cq, wuq_ref[...], preferred_element_type=F32)
    q_sw = jnp.dot(cq, wuqs_ref[...], preferred_element_type=F32)
    kv = jnp.dot(ckv, wukv_ref[...], preferred_element_type=F32)
    lane = lax.broadcasted_iota(jnp.int32, (kv.shape[0], LANES), 1)
    ones_col = jnp.where(lane == 0, 1.0, 0.0).astype(BF16)
    for hd in range(MLA_HEADS):
        b0 = hd * QK_PAD_DIM
        q_ref[:, b0:b0 + LANES] = q_all[:, b0:b0 + LANES].astype(BF16)
        q_ref[:, b0 + LANES:b0 + 2 * LANES] = (
            q_all[:, b0 + LANES:b0 + 2 * LANES] * cos + q_sw[:, hd * LANES:(hd + 1) * LANES] * sin
        ).astype(BF16)
        k_ref[:, b0:b0 + LANES] = kv[:, b0:b0 + LANES].astype(BF16)
        k_ref[:, b0 + LANES:b0 + 2 * LANES] = k_rope
        v_ref[:, b0:b0 + LANES] = kv[:, b0 + LANES:b0 + 2 * LANES].astype(BF16)
        v_ref[:, b0 + LANES:b0 + 2 * LANES] = ones_col


def _pad_rope_cols(w, swap):
    half = QK_ROPE_DIM // 2
    a, b = w[..., :half], w[..., half:]
    if swap:
        a, b = b, a
    return jnp.concatenate([a, b, jnp.zeros(w.shape[:-1] + (LANES - QK_ROPE_DIM,), w.dtype)], axis=-1)


def _mla_weights(w_in, w_uq, w_ukv, w_o):
    n = w_in.shape[0]
    lat = Q_LORA_RANK + KV_LORA_RANK
    w_kr = w_in[..., lat:]
    win_ext = jnp.concatenate(
        [w_in[..., :lat], _pad_rope_cols(w_kr, False), _pad_rope_cols(w_kr, True)], axis=-1).astype(BF16)
    wq = w_uq.reshape(n, Q_LORA_RANK, MLA_HEADS, QK_NOPE_DIM + QK_ROPE_DIM)
    rope = wq[..., QK_NOPE_DIM:]
    wuq_ext = jnp.concatenate([wq[..., :QK_NOPE_DIM], _pad_rope_cols(rope, False)], axis=-1)
    wuq_ext = wuq_ext.reshape(n, Q_LORA_RANK, MLA_HEADS * QK_PAD_DIM).astype(BF16)
    wuq_sw = _pad_rope_cols(rope, True).reshape(n, Q_LORA_RANK, MLA_HEADS * LANES).astype(BF16)
    return win_ext, wuq_ext, wuq_sw, w_ukv.astype(BF16), w_o.astype(BF16)


def _mla_proj(x, cos, sin, gains, layer, j, win_ext, q_norm, kv_norm, wuq_ext, wuq_sw, w_ukv, *, tm=512):
    rows, d = x.shape
    half = QK_ROPE_DIM // 2
    sign = np.zeros((1, LANES), np.float32)
    sign[0, :half] = -1.0
    sign[0, half:2 * half] = 1.0
    q_scale = (QK_NOPE_DIM + QK_ROPE_DIM) ** -0.5 * math.log2(math.e)
    hq = MLA_HEADS * QK_PAD_DIM
    out = jax.ShapeDtypeStruct((rows, hq), BF16)
    out_tile = pl.BlockSpec((tm, hq), lambda i: (i, 0))
    return pl.pallas_call(
        functools.partial(_mla_proj_kernel, q_scale=q_scale),
        out_shape=(out, out, out),
        grid=(rows // tm,),
        in_specs=[
            pl.BlockSpec((tm, d), lambda i: (i, 0)),
            pl.BlockSpec((tm, LANES), lambda i: (i, 0)),
            pl.BlockSpec((tm, LANES), lambda i: (i, 0)),
            _stacked(gains, layer, 2),
            _stacked(win_ext, j),
            _stacked(q_norm, j),
            _stacked(kv_norm, j),
            _stacked(wuq_ext, j),
            _stacked(wuq_sw, j),
            _stacked(w_ukv, j),
            _resident((1, LANES)),
        ],
        out_specs=(out_tile, out_tile, out_tile),
        compiler_params=_params("parallel"),
        name="mla_proj",
    )(x, cos, sin, gains, win_ext, q_norm, kv_norm, wuq_ext, wuq_sw, w_ukv, jnp.asarray(sign))


_NT = (((1,), (1,)), ((), ()))


ATTN_SUB_TILES = 2


def _attn_kernel(q_ref, k_ref, v_ref, o_ref, *, tk):
    tq = q_ref.shape[0] // ATTN_SUB_TILES
    n_kv = k_ref.shape[0] // tk
    for sub in range(ATTN_SUB_TILES):
        rows = slice(sub * tq, (sub + 1) * tq)
        q = q_ref[rows, :]

        def scores(j, q=q):
            return lax.dot_general(q, k_ref[j * tk:(j + 1) * tk, :], _NT, preferred_element_type=F32)

        m = jnp.full((tq, 1), -jnp.inf, F32)
        acc = jnp.zeros((tq, 2 * V_HEAD_DIM), F32)
        s_next = scores(0)
        for j in range(n_kv):
            s = s_next
            if j + 1 < n_kv:
                s_next = scores(j + 1)
            m_new = jnp.maximum(m, jnp.max(s, axis=-1, keepdims=True))
            alpha = jnp.exp2(m - m_new)
            p = jnp.exp2(s - m_new).astype(BF16)
            acc = alpha * acc + jnp.dot(p, v_ref[j * tk:(j + 1) * tk, :], preferred_element_type=F32)
            m = m_new
        o_ref[rows, :] = (acc[:, :V_HEAD_DIM] / acc[:, V_HEAD_DIM:V_HEAD_DIM + 1]).astype(o_ref.dtype)


def _attention(q, k, v, batch, seq, *, tq=1024, tk=1024):
    rows = q.shape[0]
    tq = min(tq, seq)
    tk = min(tk, seq)
    nq = seq // tq
    return pl.pallas_call(
        functools.partial(_attn_kernel, tk=tk),
        out_shape=jax.ShapeDtypeStruct((rows, MLA_HEADS * V_HEAD_DIM), BF16),
        grid=(batch, MLA_HEADS, nq),
        in_specs=[
            pl.BlockSpec((tq, QK_PAD_DIM), lambda b, h, i: (b * nq + i, h)),
            pl.BlockSpec((seq, QK_PAD_DIM), lambda b, h, i: (b, h)),
            pl.BlockSpec((seq, 2 * V_HEAD_DIM), lambda b, h, i: (b, h)),
        ],
        out_specs=pl.BlockSpec((tq, V_HEAD_DIM), lambda b, h, i: (b * nq + i, h)),
        compiler_params=_params("parallel", "parallel", "parallel"),
        name="mla_attn",
    )(q, k, v)


HGRN_PROJ_SUB_TILES = 4


def _hgrn_proj_kernel(x_ref, g_ref, w_ref, lbp_ref, qs_ref, hi_ref, lo_ref, kk_ref, v_ref, gate_ref, *,
                      layer_idx):
    d = x_ref.shape[1]
    sub = x_ref.shape[0] // HGRN_PROJ_SUB_TILES
    rows = [slice(s * sub, (s + 1) * sub) for s in range(HGRN_PROJ_SUB_TILES)]
    hs = [_rms(x_ref[r, :], g_ref[...]).astype(BF16) for r in rows]

    def proj(h, j):
        return jnp.dot(h, w_ref[:, j * d:(j + 1) * d], preferred_element_type=F32)

    for r, h in zip(rows, hs):
        qs_ref[r, :] = _silu(proj(h, 0)).astype(BF16)
    for direction in range(2):
        lbp = lbp_ref[direction]
        e = jnp.exp(lbp - jnp.max(lbp, axis=0, keepdims=True))
        sm = e / jnp.sum(e, axis=0, keepdims=True)
        lb = sm[1:2]
        for layer in range(2, layer_idx + 1):
            lb = lb + sm[layer:layer + 1]
        for r, h in zip(rows, hs):
            sg = 1.0 / (1.0 + jnp.exp(-proj(h, 1 + direction)))
            f = lb + (1.0 - lb) * sg
            logf2 = jnp.log2(f)
            hi = logf2.astype(BF16)
            hi_ref[direction, r, :] = hi
            lo_ref[direction, r, :] = (logf2 - hi.astype(F32)).astype(BF16)
            kk_ref[direction, r, :] = (1.0 - f).astype(BF16)
    for r, h in zip(rows, hs):
        v_ref[r, :] = proj(h, 3).astype(BF16)
    for r, h in zip(rows, hs):
        gate_ref[r, :] = _silu(proj(h, 4)).astype(BF16)


def _hgrn_proj(x, gains, layer, j, w_in, lower_bound, *, tm=512):
    rows, d = x.shape
    one = jax.ShapeDtypeStruct((rows, d), BF16)
    two = jax.ShapeDtypeStruct((2, rows, d), BF16)
    spec1 = pl.BlockSpec((tm, d), lambda i: (i, 0))
    spec2 = pl.BlockSpec((2, tm, d), lambda i: (0, i, 0))
    return pl.pallas_call(
        functools.partial(_hgrn_proj_kernel, layer_idx=layer),
        out_shape=(one, two, two, two, one, one),
        grid=(rows // tm,),
        in_specs=[spec1, _stacked(gains, layer, 2), _stacked(w_in, j), _resident(lower_bound.shape)],
        out_specs=(spec1, spec2, spec2, spec2, spec1, spec1),
        compiler_params=_params("parallel"),
        name="hgrn_proj",
    )(x, gains, w_in, lower_bound)


SCAN_EXP_BLOCKS = SCAN_LEVELS + 2
SCAN_TOTAL_ROWS = BF16_ROWS
SCAN_PIPELINE_DEPTH = 4
SCAN_XT_SLOTS = SCAN_LEVELS + 2


def _scan_exponent_matrix():
    c = SCAN_CHUNK
    out = np.zeros((2, SCAN_EXP_BLOCKS * c + SCAN_TOTAL_ROWS, c), np.float32)
    r = np.arange(c)
    for t in range(c):
        for lv in range(SCAN_LEVELS):
            mid = ((t >> lv) | 1) << lv
            if (t >> lv) & 1:
                out[0, lv * c + t, (r >= mid) & (r <= t)] = 1.0
                out[1, lv * c + t, (r >= mid) & (r < t)] = 1.0
            else:
                out[0, lv * c + t, (r > t) & (r < mid)] = 1.0
                out[1, lv * c + t, (r >= t) & (r < mid)] = 1.0
        out[0, SCAN_LEVELS * c + t, r <= t] = 1.0
        out[1, SCAN_LEVELS * c + t, r >= t] = 1.0
        out[0, (SCAN_LEVELS + 1) * c + t, r > t] = 1.0
        out[1, (SCAN_LEVELS + 1) * c + t, r < t] = 1.0
    out[:, SCAN_EXP_BLOCKS * c:, :] = 1.0
    return np.concatenate([out, out], axis=2)


def _scan_prepare(direction, qs_ref, hi_ref, lo_ref, kk_ref, a_ref, xt_scr):
    c = SCAN_CHUNK
    q_bit = 1 - direction
    hl = jnp.concatenate([hi_ref[...], lo_ref[...]], axis=0)
    w = jnp.exp2(jnp.dot(a_ref[direction], hl, preferred_element_type=F32))
    qs = qs_ref[...].astype(F32)
    kk = kk_ref[...].astype(F32)
    n_feat = qs.shape[1]
    xs = []
    for lv in range(SCAN_LEVELS):
        hs = 1 << lv
        if hs >= SUBLANES:
            base = jnp.concatenate(
                [(qs if ((r // hs) & 1) == q_bit else kk)[r:r + hs] for r in range(0, c, hs)], axis=0)
        else:
            sub = lax.broadcasted_iota(jnp.int32, (1, SUBLANES, n_feat), 1)
            pick_q = ((sub >> lv) & 1) == q_bit
            base = jnp.where(pick_q, qs.reshape(c // SUBLANES, SUBLANES, n_feat),
                             kk.reshape(c // SUBLANES, SUBLANES, n_feat)).reshape(c, n_feat)
        xs.append((w[lv * c:(lv + 1) * c] * base).astype(BF16))
        xt_scr[direction, lv] = xs[-1].T
    q_in = (qs * w[SCAN_LEVELS * c:(SCAN_LEVELS + 1) * c]).astype(BF16)
    k_out = (kk * w[(SCAN_LEVELS + 1) * c:(SCAN_LEVELS + 2) * c]).astype(BF16)
    xt_scr[direction, SCAN_LEVELS] = kk_ref[...].T
    xt_scr[direction, SCAN_LEVELS + 1] = k_out.T
    decay = w[SCAN_EXP_BLOCKS * c:SCAN_EXP_BLOCKS * c + SCAN_TOTAL_ROWS].T[:, 0:1]
    return xs, q_in, decay


def _scan_masks(direction):
    c = SCAN_CHUNK
    q_bit = 1 - direction
    t_idx = lax.broadcasted_iota(jnp.int32, (c, c), 0)
    s_idx = lax.broadcasted_iota(jnp.int32, (c, c), 1)
    diff_bits = t_idx ^ s_idx
    causal = (t_idx > s_idx) if direction == 0 else (t_idx < s_idx)
    masks = {"diag": t_idx == s_idx, "full": {}, "q_rows": {}, "q_masks": {}}
    for lv in range(SCAN_LEVELS):
        hs = 1 << lv
        if hs < BF16_ROWS:
            masks["full"][lv] = ((diff_bits >> lv) == 1) & causal
        else:
            rows = [r for r in range(0, c, hs) if ((r // hs) & 1) == q_bit]
            masks["q_rows"][lv] = rows
            masks["q_masks"][lv] = [(diff_bits[r:r + hs] >> lv) == 1 for r in rows]
    return masks


def _scan_level_products(direction, hd, xs, qb, masks, xt_scr):
    c = SCAN_CHUNK
    sl = slice(hd * c, (hd + 1) * c)
    prods = [jnp.dot(qb[:, sl], xt_scr[direction, SCAN_LEVELS, sl, :], preferred_element_type=F32)]
    for lv in range(SCAN_LEVELS):
        x = xs[lv][:, sl]
        xt = xt_scr[direction, lv, sl, :]
        if lv in masks["full"]:
            prods.append(jnp.dot(x, xt, preferred_element_type=F32))
        else:
            hs = 1 << lv
            xq = jnp.concatenate([x[r:r + hs] for r in masks["q_rows"][lv]], axis=0)
            prods.append(jnp.dot(xq, xt, preferred_element_type=F32))
    return prods


def _scan_head_output(direction, hd, prods, masks, v, q_in, decay, o_ref, st_scr, xt_scr):
    c = SCAN_CHUNK
    sl = slice(hd * c, (hd + 1) * c)
    sc = jnp.where(masks["diag"], prods[0], 0.0)
    for lv in range(SCAN_LEVELS):
        g = prods[lv + 1]
        if lv in masks["full"]:
            sc = jnp.where(masks["full"][lv], g, sc)
        else:
            hs = 1 << lv
            rows = masks["q_rows"][lv]
            parts = []
            for r in range(0, c, hs):
                if r in rows:
                    qi = rows.index(r)
                    parts.append(jnp.where(masks["q_masks"][lv][qi], g[qi * hs:(qi + 1) * hs], sc[r:r + hs]))
                else:
                    parts.append(sc[r:r + hs])
            sc = jnp.concatenate(parts, axis=0)
    st = st_scr[direction, hd]
    o = jnp.dot(sc.astype(BF16), v[:, sl], preferred_element_type=F32)
    o = o + jnp.dot(q_in[:, sl], st.astype(BF16), preferred_element_type=F32)
    o_ref[:, sl] = o.astype(o_ref.dtype)
    st_scr[direction, hd] = st * decay[sl] + jnp.dot(xt_scr[direction, SCAN_LEVELS + 1, sl, :], v[:, sl],
                                                     preferred_element_type=F32)


def _hgrn_scan_kernel(qs_f, hi_f, lo_f, kk_f, v_f, qs_b, hi_b, lo_b, kk_b, v_b, a_ref, o_f, o_b,
                      st_scr, xt_scr):
    @pl.when(pl.program_id(1) == 0)
    def _():
        st_scr[...] = jnp.zeros_like(st_scr)

    refs = ((qs_f, hi_f, lo_f, kk_f, v_f, o_f), (qs_b, hi_b, lo_b, kk_b, v_b, o_b))
    prep = [_scan_prepare(d, refs[d][0], refs[d][1], refs[d][2], refs[d][3], a_ref, xt_scr) for d in range(2)]
    masks = [_scan_masks(d) for d in range(2)]
    n_heads = qs_f.shape[1] // SCAN_CHUNK
    units = [(d, hd) for hd in range(n_heads) for d in range(2)]
    pending = []

    def finish(unit, prods):
        d, hd = unit
        _, q_in, decay = prep[d]
        _scan_head_output(d, hd, prods, masks[d], refs[d][4][...], q_in, decay, refs[d][5], st_scr, xt_scr)

    for unit in units:
        d, hd = unit
        pending.append((unit, _scan_level_products(d, hd, prep[d][0], refs[d][0][...], masks[d], xt_scr)))
        if len(pending) > SCAN_PIPELINE_DEPTH:
            finish(*pending.pop(0))
    while pending:
        finish(*pending.pop(0))


def _hgrn_scan(qs, hi, lo, kk, v, batch, seq):
    rows, f = qs.shape
    c = SCAN_CHUNK
    nc = seq // c
    a_mat = jnp.asarray(_scan_exponent_matrix(), dtype=BF16)

    def fwd(b, j):
        return b * nc + j

    def bwd(b, j):
        return b * nc + nc - 1 - j

    def specs(chunk, direction):
        one = pl.BlockSpec((c, f), lambda b, j: (chunk(b, j), 0))
        two = pl.BlockSpec((None, c, f), lambda b, j: (direction, chunk(b, j), 0))
        return [one, two, two, two, one]

    out = jax.ShapeDtypeStruct((rows, f), BF16)
    return pl.pallas_call(
        _hgrn_scan_kernel,
        out_shape=(out, out),
        grid=(batch, nc),
        in_specs=specs(fwd, 0) + specs(bwd, 1) + [_resident(a_mat.shape)],
        out_specs=(pl.BlockSpec((c, f), lambda b, j: (fwd(b, j), 0)),
                   pl.BlockSpec((c, f), lambda b, j: (bwd(b, j), 0))),
        scratch_shapes=[pltpu.VMEM((2, HGRN_HEADS, c, c), F32),
                        pltpu.VMEM((2, SCAN_XT_SLOTS, f, c), BF16)],
        compiler_params=_params("parallel", "arbitrary"),
        name="hgrn_scan",
    )(qs, hi, lo, kk, v, qs, hi, lo, kk, v, a_mat)


def kernel(x, positions, norm_gains, ffn_w_in, ffn_w_out, mla_w_in, mla_q_norm, mla_kv_norm, mla_w_uq,
           mla_w_ukv, mla_w_o, hgrn_w_in, hgrn_lower_bound, hgrn_out_norm, hgrn_w_o):
    batch, seq, d = x.shape
    depth = norm_gains.shape[0]
    rows = batch * seq
    xf = x.reshape(rows, d)
    cos, sin = _rope_tables(positions)
    gains = norm_gains.reshape(depth, N_NORMS_PER_LAYER, 1, d)
    ffn_w = ffn_w_in.astype(BF16)
    wout = ffn_w_out.astype(BF16)
    win_ext, wuq_ext, wuq_sw, wukv, mla_wo = _mla_weights(mla_w_in, mla_w_uq, mla_w_ukv, mla_w_o)
    q_norm = mla_q_norm.reshape(-1, 1, Q_LORA_RANK)
    kv_norm = mla_kv_norm.reshape(-1, 1, KV_LORA_RANK)
    hgrn_win = hgrn_w_in.astype(BF16)
    hgrn_wo = hgrn_w_o.astype(BF16)
    out_norm = hgrn_out_norm.reshape(-1, 1, d)
    for l in range(depth):
        xf = _ffn(xf, gains, ffn_w, wout, l, 0)
        j = l // N_MIXERS
        if l % N_MIXERS == 0:
            q, k, v = _mla_proj(xf, cos, sin, gains, l, j, win_ext, q_norm, kv_norm, wuq_ext, wuq_sw, wukv)
            o = _attention(q, k, v, batch, seq)
            xf = _mla_out_ffn(xf, o, mla_wo, j, gains, ffn_w, wout, l)
        else:
            qs, hi, lo, kk, v, gate = _hgrn_proj(xf, gains, l, j, hgrn_win, hgrn_lower_bound)
            o_f, o_b = _hgrn_scan(qs, hi, lo, kk, v, batch, seq)
            xf = _hgrn_out_ffn(xf, o_f, o_b, gate, out_norm, hgrn_wo, j, gains, ffn_w, wout, l)
    return xf.reshape(batch, seq, d)
```

```python
import functools
import math

import numpy as np
import jax
import jax.numpy as jnp
from jax import lax
from jax.experimental import pallas as pl
from jax.experimental.pallas import tpu as pltpu

EPS = 1e-6
ROPE_THETA = 10000.0
MLA_HEADS = 8
QK_NOPE_DIM = 128
QK_ROPE_DIM = 64
V_HEAD_DIM = 128
Q_LORA_RANK = 256
KV_LORA_RANK = 128
HGRN_HEADS = 8
N_MIXERS = 2
N_NORMS_PER_LAYER = 6

LANES = 128
SUBLANES = 8
BF16_ROWS = 16
QK_PAD_DIM = 256
SCAN_CHUNK = 128
SCAN_LEVELS = 7
VMEM_LIMIT = 56 * 1024 * 1024

BF16 = jnp.bfloat16
F32 = jnp.float32


def _params(*semantics):
    return pltpu.CompilerParams(dimension_semantics=semantics, vmem_limit_bytes=VMEM_LIMIT)


def _resident(shape):
    nd = len(shape)
    return pl.BlockSpec(shape, lambda *_: (0,) * nd, pipeline_mode=pl.Buffered(1))


def _stacked(arr, *lead):
    tail = arr.shape[len(lead):]
    index = tuple(lead) + (0,) * len(tail)
    return pl.BlockSpec((None,) * len(lead) + tail, lambda *_: index, pipeline_mode=pl.Buffered(1))


def _rms(y, gain):
    return y * lax.rsqrt(jnp.mean(y * y, axis=-1, keepdims=True) + EPS) * gain


def _silu(t):
    return t * (1.0 / (1.0 + jnp.exp(-t)))


FFN_CHUNK = 256
MIXER_FFN_SUB_TILES = 2


def _sub_rows(ref, n):
    sub = ref.shape[0] // n
    return [slice(s * sub, (s + 1) * sub) for s in range(n)]


def _ffn_body(xs, rows, gin_ref, gout_ref, w_ref, wout_ref, o_ref, a_scr):
    d_ff = wout_ref.shape[0]
    hs = [_rms(x, gin_ref[...]).astype(BF16) for x in xs]
    for lo in range(0, d_ff, FFN_CHUNK):
        for r, h in zip(rows, hs):
            gate = jnp.dot(h, w_ref[:, lo:lo + FFN_CHUNK], preferred_element_type=F32)
            up = jnp.dot(h, w_ref[:, d_ff + lo:d_ff + lo + FFN_CHUNK], preferred_element_type=F32)
            a_scr[r, lo:lo + FFN_CHUNK] = (_silu(gate) * up).astype(BF16)
    for r, x in zip(rows, xs):
        y = jnp.dot(a_scr[r, :], wout_ref[...], preferred_element_type=F32)
        o_ref[r, :] = x + 0.5 * _rms(y, gout_ref[...])


def _ffn_kernel(x_ref, *ffn_refs):
    rows = _sub_rows(x_ref, 1)
    _ffn_body([x_ref[r, :] for r in rows], rows, *ffn_refs)


def _mla_out_ffn_kernel(x_ref, attn_ref, wo_ref, g_ref, *ffn_refs):
    rows = _sub_rows(x_ref, MIXER_FFN_SUB_TILES)
    xs = []
    for r in rows:
        m = jnp.dot(attn_ref[r, :], wo_ref[...], preferred_element_type=F32)
        xs.append(x_ref[r, :] + _rms(m, g_ref[...]))
    _ffn_body(xs, rows, *ffn_refs)


def _hgrn_out_ffn_kernel(x_ref, of_ref, ob_ref, gate_ref, on_ref, wo_ref, g_ref, *ffn_refs):
    rows = _sub_rows(x_ref, MIXER_FFN_SUB_TILES)
    hd_w = x_ref.shape[1] // HGRN_HEADS
    xs = []
    for r in rows:
        o = of_ref[r, :].astype(F32) + ob_ref[r, :].astype(F32)
        gate = gate_ref[r, :].astype(F32)
        parts = []
        for hd in range(HGRN_HEADS):
            sl = slice(hd * hd_w, (hd + 1) * hd_w)
            parts.append((_rms(o[:, sl], on_ref[:, sl]) * gate[:, sl]).astype(BF16))
        m = jnp.dot(jnp.concatenate(parts, axis=1), wo_ref[...], preferred_element_type=F32)
        xs.append(x_ref[r, :] + _rms(m, g_ref[...]))
    _ffn_body(xs, rows, *ffn_refs)


def _ffn_call(kernel_fn, name, x, mixer_args, mixer_specs, gains, w, wout, layer, which, *, tm=512):
    rows, d = x.shape
    tile = pl.BlockSpec((tm, d), lambda i: (i, 0))
    return pl.pallas_call(
        kernel_fn,
        out_shape=jax.ShapeDtypeStruct((rows, d), F32),
        grid=(rows // tm,),
        in_specs=[tile] + mixer_specs + [
            _stacked(gains, layer, 4 * which), _stacked(gains, layer, 4 * which + 1),
            _stacked(w, layer, which), _stacked(wout, layer, which)],
        out_specs=tile,
        scratch_shapes=[pltpu.VMEM((tm, wout.shape[-2]), BF16)],
        compiler_params=_params("parallel"),
        name=name,
    )(x, *mixer_args, gains, gains, w, wout)


def _ffn(x, gains, w, wout, layer, which):
    return _ffn_call(_ffn_kernel, "ffn", x, [], [], gains, w, wout, layer, which)


def _mla_out_ffn(x, attn, w_o, j, gains, w, wout, layer, *, tm=512):
    specs = [pl.BlockSpec((tm, attn.shape[1]), lambda i: (i, 0)), _stacked(w_o, j), _stacked(gains, layer, 3)]
    return _ffn_call(_mla_out_ffn_kernel, "mla_out_ffn", x, [attn, w_o, gains], specs, gains, w, wout,
                     layer, 1, tm=tm)


def _hgrn_out_ffn(x, o_f, o_b, gate, out_norm, w_o, j, gains, w, wout, layer, *, tm=512):
    tile = pl.BlockSpec((tm, x.shape[1]), lambda i: (i, 0))
    specs = [tile, tile, tile, _stacked(out_norm, j), _stacked(w_o, j), _stacked(gains, layer, 3)]
    return _ffn_call(_hgrn_out_ffn_kernel, "hgrn_out_ffn", x, [o_f, o_b, gate, out_norm, w_o, gains], specs,
                     gains, w, wout, layer, 1, tm=tm)


def _rope_kernel(pos_ref, invf_ref, cos_ref, sin_ref):
    ang = pos_ref[...].astype(F32) * invf_ref[...]
    cos_ref[...] = jnp.cos(ang)
    sin_ref[...] = jnp.sin(ang)


def _rope_tables(positions):
    rows = positions.size
    half = QK_ROPE_DIM // 2
    per_row = LANES // half
    inv_freq = ROPE_THETA ** (-np.arange(0, QK_ROPE_DIM, 2, dtype=np.float32) / QK_ROPE_DIM)
    invf = jnp.asarray(np.tile(inv_freq, per_row)[None, :])
    pos = jnp.repeat(positions.reshape(rows // per_row, per_row), half, axis=1)
    tm = min(1024, rows // per_row)
    tile = pl.BlockSpec((tm, LANES), lambda i: (i, 0))
    table = jax.ShapeDtypeStruct((rows // per_row, LANES), F32)
    cos, sin = pl.pallas_call(
        _rope_kernel,
        out_shape=(table, table),
        grid=(rows // per_row // tm,),
        in_specs=[tile, _resident((1, LANES))],
        out_specs=(tile, tile),
        compiler_params=_params("parallel"),
        name="rope_tables",
    )(pos, invf)

    def per_token(t):
        t = t.reshape(rows, half)
        return jnp.concatenate([t, t, jnp.zeros((rows, LANES - QK_ROPE_DIM), F32)], axis=1)

    return per_token(cos), per_token(sin)


def _mla_proj_kernel(x_ref, cos_ref, sin_ref, g_ref, win_ref, qn_ref, kvn_ref, wuq_ref, wuqs_ref, wukv_ref,
                     sign_ref, q_ref, k_ref, v_ref, *, q_scale):
    h = _rms(x_ref[...], g_ref[...]).astype(BF16)
    p = jnp.dot(h, win_ref[...], preferred_element_type=F32)
    cq = _rms(p[:, :Q_LORA_RANK], qn_ref[...] * q_scale).astype(BF16)
    ckv = _rms(p[:, Q_LORA_RANK:Q_LORA_RANK + KV_LORA_RANK], kvn_ref[...]).astype(BF16)
    kr = p[:, Q_LORA_RANK + KV_LORA_RANK:Q_LORA_RANK + KV_LORA_RANK + LANES]
    kr_sw = p[:, Q_LORA_RANK + KV_LORA_RANK + LANES:]
    cos = cos_ref[...]
    sin = sin_ref[...] * sign_ref[...]
    k_rope = (kr * cos + kr_sw * sin).astype(BF16)
    q_all = jnp.dot(cq, wuq_ref[...], preferred_element_type=F32)
    q_sw = jnp.dot(cq, wuqs_ref[...], preferred_element_type=F32)
    kv = jnp.dot(ckv, wukv_ref[...], preferred_element_type=F32)
    lane = lax.broadcasted_iota(jnp.int32, (kv.shape[0], LANES), 1)
    ones_col = jnp.where(lane == 0, 1.0, 0.0).astype(BF16)
    for hd in range(MLA_HEADS):
        b0 = hd * QK_PAD_DIM
        q_ref[:, b0:b0 + LANES] = q_all[:, b0:b0 + LANES].astype(BF16)
        q_ref[:, b0 + LANES:b0 + 2 * LANES] = (
            q_all[:, b0 + LANES:b0 + 2 * LANES] * cos + q_sw[:, hd * LANES:(hd + 1) * LANES] * sin
        ).astype(BF16)
        k_ref[:, b0:b0 + LANES] = kv[:, b0:b0 + LANES].astype(BF16)
        k_ref[:, b0 + LANES:b0 + 2 * LANES] = k_rope
        v_ref[:, b0:b0 + LANES] = kv[:, b0 + LANES:b0 + 2 * LANES].astype(BF16)
        v_ref[:, b0 + LANES:b0 + 2 * LANES] = ones_col


def _pad_rope_cols(w, swap):
    half = QK_ROPE_DIM // 2
    a, b = w[..., :half], w[..., half:]
    if swap:
        a, b = b, a
    return jnp.concatenate([a, b, jnp.zeros(w.shape[:-1] + (LANES - QK_ROPE_DIM,), w.dtype)], axis=-1)


def _mla_weights(w_in, w_uq, w_ukv, w_o):
    n = w_in.shape[0]
    lat = Q_LORA_RANK + KV_LORA_RANK
    w_kr = w_in[..., lat:]
    win_ext = jnp.concatenate(
        [w_in[..., :lat], _pad_rope_cols(w_kr, False), _pad_rope_cols(w_kr, True)], axis=-1).astype(BF16)
    wq = w_uq.reshape(n, Q_LORA_RANK, MLA_HEADS, QK_NOPE_DIM + QK_ROPE_DIM)
    rope = wq[..., QK_NOPE_DIM:]
    wuq_ext = jnp.concatenate([wq[..., :QK_NOPE_DIM], _pad_rope_cols(rope, False)], axis=-1)
    wuq_ext = wuq_ext.reshape(n, Q_LORA_RANK, MLA_HEADS * QK_PAD_DIM).astype(BF16)
    wuq_sw = _pad_rope_cols(rope, True).reshape(n, Q_LORA_RANK, MLA_HEADS * LANES).astype(BF16)
    return win_ext, wuq_ext, wuq_sw, w_ukv.astype(BF16), w_o.astype(BF16)


def _mla_proj(x, cos, sin, gains, layer, j, win_ext, q_norm, kv_norm, wuq_ext, wuq_sw, w_ukv, *, tm=512):
    rows, d = x.shape
    half = QK_ROPE_DIM // 2
    sign = np.zeros((1, LANES), np.float32)
    sign[0, :half] = -1.0
    sign[0, half:2 * half] = 1.0
    q_scale = (QK_NOPE_DIM + QK_ROPE_DIM) ** -0.5 * math.log2(math.e)
    hq = MLA_HEADS * QK_PAD_DIM
    out = jax.ShapeDtypeStruct((rows, hq), BF16)
    out_tile = pl.BlockSpec((tm, hq), lambda i: (i, 0))
    return pl.pallas_call(
        functools.partial(_mla_proj_kernel, q_scale=q_scale),
        out_shape=(out, out, out),
        grid=(rows // tm,),
        in_specs=[
            pl.BlockSpec((tm, d), lambda i: (i, 0)),
            pl.BlockSpec((tm, LANES), lambda i: (i, 0)),
            pl.BlockSpec((tm, LANES), lambda i: (i, 0)),
            _stacked(gains, layer, 2),
            _stacked(win_ext, j),
            _stacked(q_norm, j),
            _stacked(kv_norm, j),
            _stacked(wuq_ext, j),
            _stacked(wuq_sw, j),
            _stacked(w_ukv, j),
            _resident((1, LANES)),
        ],
        out_specs=(out_tile, out_tile, out_tile),
        compiler_params=_params("parallel"),
        name="mla_proj",
    )(x, cos, sin, gains, win_ext, q_norm, kv_norm, wuq_ext, wuq_sw, w_ukv, jnp.asarray(sign))


_NT = (((1,), (1,)), ((), ()))


ATTN_SUB_TILES = 2


def _attn_kernel(q_ref, k_ref, v_ref, o_ref, *, tk):
    tq = q_ref.shape[0] // ATTN_SUB_TILES
    n_kv = k_ref.shape[0] // tk
    for sub in range(ATTN_SUB_TILES):
        rows = slice(sub * tq, (sub + 1) * tq)
        q = q_ref[rows, :]

        def scores(j, q=q):
            return lax.dot_general(q, k_ref[j * tk:(j + 1) * tk, :], _NT, preferred_element_type=F32)

        m = jnp.full((tq, 1), -jnp.inf, F32)
        acc = jnp.zeros((tq, 2 * V_HEAD_DIM), F32)
        s_next = scores(0)
        for j in range(n_kv):
            s = s_next
            if j + 1 < n_kv:
                s_next = scores(j + 1)
            m_new = jnp.maximum(m, jnp.max(s, axis=-1, keepdims=True))
            alpha = jnp.exp2(m - m_new)
            p = jnp.exp2(s - m_new).astype(BF16)
            acc = alpha * acc + jnp.dot(p, v_ref[j * tk:(j + 1) * tk, :], preferred_element_type=F32)
            m = m_new
        o_ref[rows, :] = (acc[:, :V_HEAD_DIM] / acc[:, V_HEAD_DIM:V_HEAD_DIM + 1]).astype(o_ref.dtype)


def _attention(q, k, v, batch, seq, *, tq=1024, tk=2048):
    rows = q.shape[0]
    tq = min(tq, seq)
    tk = min(tk, seq)
    nq = seq // tq
    return pl.pallas_call(
        functools.partial(_attn_kernel, tk=tk),
        out_shape=jax.ShapeDtypeStruct((rows, MLA_HEADS * V_HEAD_DIM), BF16),
        grid=(batch, MLA_HEADS, nq),
        in_specs=[
            pl.BlockSpec((tq, QK_PAD_DIM), lambda b, h, i: (b * nq + i, h)),
            pl.BlockSpec((seq, QK_PAD_DIM), lambda b, h, i: (b, h)),
            pl.BlockSpec((seq, 2 * V_HEAD_DIM), lambda b, h, i: (b, h)),
        ],
        out_specs=pl.BlockSpec((tq, V_HEAD_DIM), lambda b, h, i: (b * nq + i, h)),
        compiler_params=_params("parallel", "parallel", "parallel"),
        name="mla_attn",
    )(q, k, v)


HGRN_PROJ_SUB_TILES = 4


def _hgrn_proj_kernel(x_ref, g_ref, w_ref, lbp_ref, qs_ref, hi_ref, lo_ref, kk_ref, v_ref, gate_ref, *,
                      layer_idx):
    d = x_ref.shape[1]
    sub = x_ref.shape[0] // HGRN_PROJ_SUB_TILES
    rows = [slice(s * sub, (s + 1) * sub) for s in range(HGRN_PROJ_SUB_TILES)]
    hs = [_rms(x_ref[r, :], g_ref[...]).astype(BF16) for r in rows]

    def proj(h, j):
        return jnp.dot(h, w_ref[:, j * d:(j + 1) * d], preferred_element_type=F32)

    for r, h in zip(rows, hs):
        qs_ref[r, :] = _silu(proj(h, 0)).astype(BF16)
    for direction in range(2):
        lbp = lbp_ref[direction]
        e = jnp.exp(lbp - jnp.max(lbp, axis=0, keepdims=True))
        sm = e / jnp.sum(e, axis=0, keepdims=True)
        lb = sm[1:2]
        for layer in range(2, layer_idx + 1):
            lb = lb + sm[layer:layer + 1]
        for r, h in zip(rows, hs):
            sg = 1.0 / (1.0 + jnp.exp(-proj(h, 1 + direction)))
            f = lb + (1.0 - lb) * sg
            logf2 = jnp.log2(f)
            hi = logf2.astype(BF16)
            hi_ref[direction, r, :] = hi
            lo_ref[direction, r, :] = (logf2 - hi.astype(F32)).astype(BF16)
            kk_ref[direction, r, :] = (1.0 - f).astype(BF16)
    for r, h in zip(rows, hs):
        v_ref[r, :] = proj(h, 3).astype(BF16)
    for r, h in zip(rows, hs):
        gate_ref[r, :] = _silu(proj(h, 4)).astype(BF16)


def _hgrn_proj(x, gains, layer, j, w_in, lower_bound, *, tm=512):
    rows, d = x.shape
    one = jax.ShapeDtypeStruct((rows, d), BF16)
    two = jax.ShapeDtypeStruct((2, rows, d), BF16)
    spec1 = pl.BlockSpec((tm, d), lambda i: (i, 0))
    spec2 = pl.BlockSpec((2, tm, d), lambda i: (0, i, 0))
    return pl.pallas_call(
        functools.partial(_hgrn_proj_kernel, layer_idx=layer),
        out_shape=(one, two, two, two, one, one),
        grid=(rows // tm,),
        in_specs=[spec1, _stacked(gains, layer, 2), _stacked(w_in, j), _resident(lower_bound.shape)],
        out_specs=(spec1, spec2, spec2, spec2, spec1, spec1),
        compiler_params=_params("parallel"),
        name="hgrn_proj",
    )(x, gains, w_in, lower_bound)


SCAN_EXP_BLOCKS = SCAN_LEVELS + 2
SCAN_TOTAL_ROWS = BF16_ROWS
SCAN_PIPELINE_DEPTH = 4
SCAN_XT_SLOTS = SCAN_LEVELS + 2


def _scan_exponent_matrix():
    c = SCAN_CHUNK
    out = np.zeros((2, SCAN_EXP_BLOCKS * c + SCAN_TOTAL_ROWS, c), np.float32)
    r = np.arange(c)
    for t in range(c):
        for lv in range(SCAN_LEVELS):
            mid = ((t >> lv) | 1) << lv
            if (t >> lv) & 1:
                out[0, lv * c + t, (r >= mid) & (r <= t)] = 1.0
                out[1, lv * c + t, (r >= mid) & (r < t)] = 1.0
            else:
                out[0, lv * c + t, (r > t) & (r < mid)] = 1.0
                out[1, lv * c + t, (r >= t) & (r < mid)] = 1.0
        out[0, SCAN_LEVELS * c + t, r <= t] = 1.0
        out[1, SCAN_LEVELS * c + t, r >= t] = 1.0
        out[0, (SCAN_LEVELS + 1) * c + t, r > t] = 1.0
        out[1, (SCAN_LEVELS + 1) * c + t, r < t] = 1.0
    out[:, SCAN_EXP_BLOCKS * c:, :] = 1.0
    return np.concatenate([out, out], axis=2)


def _scan_prepare(direction, qs_ref, hi_ref, lo_ref, kk_ref, a_ref, xt_scr):
    c = SCAN_CHUNK
    q_bit = 1 - direction
    hl = jnp.concatenate([hi_ref[...], lo_ref[...]], axis=0)
    w = jnp.exp2(jnp.dot(a_ref[direction], hl, preferred_element_type=F32))
    qs = qs_ref[...].astype(F32)
    kk = kk_ref[...].astype(F32)
    n_feat = qs.shape[1]
    xs = []
    for lv in range(SCAN_LEVELS):
        hs = 1 << lv
        if hs >= SUBLANES:
            base = jnp.concatenate(
                [(qs if ((r // hs) & 1) == q_bit else kk)[r:r + hs] for r in range(0, c, hs)], axis=0)
        else:
            sub = lax.broadcasted_iota(jnp.int32, (1, SUBLANES, n_feat), 1)
            pick_q = ((sub >> lv) & 1) == q_bit
            base = jnp.where(pick_q, qs.reshape(c // SUBLANES, SUBLANES, n_feat),
                             kk.reshape(c // SUBLANES, SUBLANES, n_feat)).reshape(c, n_feat)
        xs.append((w[lv * c:(lv + 1) * c] * base).astype(BF16))
        xt_scr[direction, lv] = xs[-1].T
    q_in = (qs * w[SCAN_LEVELS * c:(SCAN_LEVELS + 1) * c]).astype(BF16)
    k_out = (kk * w[(SCAN_LEVELS + 1) * c:(SCAN_LEVELS + 2) * c]).astype(BF16)
    xt_scr[direction, SCAN_LEVELS] = kk_ref[...].T
    xt_scr[direction, SCAN_LEVELS + 1] = k_out.T
    decay = w[SCAN_EXP_BLOCKS * c:SCAN_EXP_BLOCKS * c + SCAN_TOTAL_ROWS].T[:, 0:1]
    return xs, q_in, decay


def _scan_masks(direction):
    c = SCAN_CHUNK
    q_bit = 1 - direction
    t_idx = lax.broadcasted_iota(jnp.int32, (c, c), 0)
    s_idx = lax.broadcasted_iota(jnp.int32, (c, c), 1)
    diff_bits = t_idx ^ s_idx
    causal = (t_idx > s_idx) if direction == 0 else (t_idx < s_idx)
    masks = {"diag": t_idx == s_idx, "full": {}, "q_rows": {}, "q_masks": {}}
    for lv in range(SCAN_LEVELS):
        hs = 1 << lv
        if hs < BF16_ROWS:
            masks["full"][lv] = ((diff_bits >> lv) == 1) & causal
        else:
            rows = [r for r in range(0, c, hs) if ((r // hs) & 1) == q_bit]
            masks["q_rows"][lv] = rows
            masks["q_masks"][lv] = [(diff_bits[r:r + hs] >> lv) == 1 for r in rows]
    return masks


def _scan_level_products(direction, hd, xs, qb, masks, xt_scr):
    c = SCAN_CHUNK
    sl = slice(hd * c, (hd + 1) * c)
    prods = [jnp.dot(qb[:, sl], xt_scr[direction, SCAN_LEVELS, sl, :], preferred_element_type=F32)]
    for lv in range(SCAN_LEVELS):
        x = xs[lv][:, sl]
        xt = xt_scr[direction, lv, sl, :]
        if lv in masks["full"]:
            prods.append(jnp.dot(x, xt, preferred_element_type=F32))
        else:
            hs = 1 << lv
            xq = jnp.concatenate([x[r:r + hs] for r in masks["q_rows"][lv]], axis=0)
            prods.append(jnp.dot(xq, xt, preferred_element_type=F32))
    return prods


def _scan_head_output(direction, hd, prods, masks, v, q_in, decay, o_ref, st_scr, xt_scr):
    c = SCAN_CHUNK
    sl = slice(hd * c, (hd + 1) * c)
    sc = jnp.where(masks["diag"], prods[0], 0.0)
    for lv in range(SCAN_LEVELS):
        g = prods[lv + 1]
        if lv in masks["full"]:
            sc = jnp.where(masks["full"][lv], g, sc)
        else:
            hs = 1 << lv
            rows = masks["q_rows"][lv]
            parts = []
            for r in range(0, c, hs):
                if r in rows:
                    qi = rows.index(r)
                    parts.append(jnp.where(masks["q_masks"][lv][qi], g[qi * hs:(qi + 1) * hs], sc[r:r + hs]))
                else:
                    parts.append(sc[r:r + hs])
            sc = jnp.concatenate(parts, axis=0)
    st = st_scr[direction, hd]
    o = jnp.dot(sc.astype(BF16), v[:, sl], preferred_element_type=F32)
    o = o + jnp.dot(q_in[:, sl], st.astype(BF16), preferred_element_type=F32)
    o_ref[:, sl] = o.astype(o_ref.dtype)
    st_scr[direction, hd] = st * decay[sl] + jnp.dot(xt_scr[direction, SCAN_LEVELS + 1, sl, :], v[:, sl],
                                                     preferred_element_type=F32)


def _hgrn_scan_kernel(qs_f, hi_f, lo_f, kk_f, v_f, qs_b, hi_b, lo_b, kk_b, v_b, a_ref, o_f, o_b,
                      st_scr, xt_scr):
    @pl.when(pl.program_id(1) == 0)
    def _():
        st_scr[...] = jnp.zeros_like(st_scr)

    refs = ((qs_f, hi_f, lo_f, kk_f, v_f, o_f), (qs_b, hi_b, lo_b, kk_b, v_b, o_b))
    prep = [_scan_prepare(d, refs[d][0], refs[d][1], refs[d][2], refs[d][3], a_ref, xt_scr) for d in range(2)]
    masks = [_scan_masks(d) for d in range(2)]
    n_heads = qs_f.shape[1] // SCAN_CHUNK
    units = [(d, hd) for hd in range(n_heads) for d in range(2)]
    pending = []

    def finish(unit, prods):
        d, hd = unit
        _, q_in, decay = prep[d]
        _scan_head_output(d, hd, prods, masks[d], refs[d][4][...], q_in, decay, refs[d][5], st_scr, xt_scr)

    for unit in units:
        d, hd = unit
        pending.append((unit, _scan_level_products(d, hd, prep[d][0], refs[d][0][...], masks[d], xt_scr)))
        if len(pending) > SCAN_PIPELINE_DEPTH:
            finish(*pending.pop(0))
    while pending:
        finish(*pending.pop(0))


def _hgrn_scan(qs, hi, lo, kk, v, batch, seq):
    rows, f = qs.shape
    c = SCAN_CHUNK
    nc = seq // c
    a_mat = jnp.asarray(_scan_exponent_matrix(), dtype=BF16)

    def fwd(b, j):
        return b * nc + j

    def bwd(b, j):
        return b * nc + nc - 1 - j

    def specs(chunk, direction):
        one = pl.BlockSpec((c, f), lambda b, j: (chunk(b, j), 0))
        two = pl.BlockSpec((None, c, f), lambda b, j: (direction, chunk(b, j), 0))
        return [one, two, two, two, one]

    out = jax.ShapeDtypeStruct((rows, f), BF16)
    return pl.pallas_call(
        _hgrn_scan_kernel,
        out_shape=(out, out),
        grid=(batch, nc),
        in_specs=specs(fwd, 0) + specs(bwd, 1) + [_resident(a_mat.shape)],
        out_specs=(pl.BlockSpec((c, f), lambda b, j: (fwd(b, j), 0)),
                   pl.BlockSpec((c, f), lambda b, j: (bwd(b, j), 0))),
        scratch_shapes=[pltpu.VMEM((2, HGRN_HEADS, c, c), F32),
                        pltpu.VMEM((2, SCAN_XT_SLOTS, f, c), BF16)],
        compiler_params=_params("parallel", "arbitrary"),
        name="hgrn_scan",
    )(qs, hi, lo, kk, v, qs, hi, lo, kk, v, a_mat)


def kernel(x, positions, norm_gains, ffn_w_in, ffn_w_out, mla_w_in, mla_q_norm, mla_kv_norm, mla_w_uq,
           mla_w_ukv, mla_w_o, hgrn_w_in, hgrn_lower_bound, hgrn_out_norm, hgrn_w_o):
    batch, seq, d = x.shape
    depth = norm_gains.shape[0]
    rows = batch * seq
    xf = x.reshape(rows, d)
    cos, sin = _rope_tables(positions)
    gains = norm_gains.reshape(depth, N_NORMS_PER_LAYER, 1, d)
    ffn_w = ffn_w_in.astype(BF16)
    wout = ffn_w_out.astype(BF16)
    win_ext, wuq_ext, wuq_sw, wukv, mla_wo = _mla_weights(mla_w_in, mla_w_uq, mla_w_ukv, mla_w_o)
    q_norm = mla_q_norm.reshape(-1, 1, Q_LORA_RANK)
    kv_norm = mla_kv_norm.reshape(-1, 1, KV_LORA_RANK)
    hgrn_win = hgrn_w_in.astype(BF16)
    hgrn_wo = hgrn_w_o.astype(BF16)
    out_norm = hgrn_out_norm.reshape(-1, 1, d)
    for l in range(depth):
        xf = _ffn(xf, gains, ffn_w, wout, l, 0)
        j = l // N_MIXERS
        if l % N_MIXERS == 0:
            q, k, v = _mla_proj(xf, cos, sin, gains, l, j, win_ext, q_norm, kv_norm, wuq_ext, wuq_sw, wukv)
            o = _attention(q, k, v, batch, seq)
            xf = _mla_out_ffn(xf, o, mla_wo, j, gains, ffn_w, wout, l)
        else:
            qs, hi, lo, kk, v, gate = _hgrn_proj(xf, gains, l, j, hgrn_win, hgrn_lower_bound)
            o_f, o_b = _hgrn_scan(qs, hi, lo, kk, v, batch, seq)
            xf = _hgrn_out_ffn(xf, o_f, o_b, gate, out_norm, hgrn_wo, j, gains, ffn_w, wout, l)
    return xf.reshape(batch, seq, d)
```

```python
import functools
import math

import numpy as np
import jax
import jax.numpy as jnp
from jax import lax
from jax.experimental import pallas as pl
from jax.experimental.pallas import tpu as pltpu

EPS = 1e-6
ROPE_THETA = 10000.0
MLA_HEADS = 8
QK_NOPE_DIM = 128
QK_ROPE_DIM = 64
V_HEAD_DIM = 128
Q_LORA_RANK = 256
KV_LORA_RANK = 128
HGRN_HEADS = 8
N_MIXERS = 2
N_NORMS_PER_LAYER = 6

LANES = 128
SUBLANES = 8
BF16_ROWS = 16
QK_PAD_DIM = 256
SCAN_CHUNK = 128
SCAN_LEVELS = 7
VMEM_LIMIT = 56 * 1024 * 1024

BF16 = jnp.bfloat16
F32 = jnp.float32


def _params(*semantics):
    return pltpu.CompilerParams(dimension_semantics=semantics, vmem_limit_bytes=VMEM_LIMIT)


def _resident(shape):
    nd = len(shape)
    return pl.BlockSpec(shape, lambda *_: (0,) * nd, pipeline_mode=pl.Buffered(1))


def _stacked(arr, *lead):
    tail = arr.shape[len(lead):]
    index = tuple(lead) + (0,) * len(tail)
    return pl.BlockSpec((None,) * len(lead) + tail, lambda *_: index, pipeline_mode=pl.Buffered(1))


def _rms(y, gain):
    return y * lax.rsqrt(jnp.mean(y * y, axis=-1, keepdims=True) + EPS) * gain


def _silu(t):
    return t * (1.0 / (1.0 + jnp.exp(-t)))


FFN_CHUNK = 256
MIXER_FFN_SUB_TILES = 2


def _sub_rows(ref, n):
    sub = ref.shape[0] // n
    return [slice(s * sub, (s + 1) * sub) for s in range(n)]


def _ffn_body(xs, rows, gin_ref, gout_ref, w_ref, wout_ref, o_ref, a_scr):
    d_ff = wout_ref.shape[0]
    hs = [_rms(x, gin_ref[...]).astype(BF16) for x in xs]
    for lo in range(0, d_ff, FFN_CHUNK):
        for r, h in zip(rows, hs):
            gate = jnp.dot(h, w_ref[:, lo:lo + FFN_CHUNK], preferred_element_type=F32)
            up = jnp.dot(h, w_ref[:, d_ff + lo:d_ff + lo + FFN_CHUNK], preferred_element_type=F32)
            a_scr[r, lo:lo + FFN_CHUNK] = (_silu(gate) * up).astype(BF16)
    for r, x in zip(rows, xs):
        y = jnp.dot(a_scr[r, :], wout_ref[...], preferred_element_type=F32)
        o_ref[r, :] = x + 0.5 * _rms(y, gout_ref[...])


def _ffn_kernel(x_ref, *ffn_refs):
    rows = _sub_rows(x_ref, 1)
    _ffn_body([x_ref[r, :] for r in rows], rows, *ffn_refs)


def _mla_out_ffn_kernel(x_ref, attn_ref, wo_ref, g_ref, *ffn_refs):
    rows = _sub_rows(x_ref, MIXER_FFN_SUB_TILES)
    xs = []
    for r in rows:
        m = jnp.dot(attn_ref[r, :], wo_ref[...], preferred_element_type=F32)
        xs.append(x_ref[r, :] + _rms(m, g_ref[...]))
    _ffn_body(xs, rows, *ffn_refs)


def _hgrn_out_ffn_kernel(x_ref, of_ref, ob_ref, gate_ref, on_ref, wo_ref, g_ref, *ffn_refs):
    rows = _sub_rows(x_ref, MIXER_FFN_SUB_TILES)
    hd_w = x_ref.shape[1] // HGRN_HEADS
    xs = []
    for r in rows:
        o = of_ref[r, :].astype(F32) + ob_ref[r, :].astype(F32)
        gate = gate_ref[r, :].astype(F32)
        parts = []
        for hd in range(HGRN_HEADS):
            sl = slice(hd * hd_w, (hd + 1) * hd_w)
            parts.append((_rms(o[:, sl], on_ref[:, sl]) * gate[:, sl]).astype(BF16))
        m = jnp.dot(jnp.concatenate(parts, axis=1), wo_ref[...], preferred_element_type=F32)
        xs.append(x_ref[r, :] + _rms(m, g_ref[...]))
    _ffn_body(xs, rows, *ffn_refs)


def _ffn_call(kernel_fn, name, x, mixer_args, mixer_specs, gains, w, wout, layer, which, *, tm=512):
    rows, d = x.shape
    tile = pl.BlockSpec((tm, d), lambda i: (i, 0))
    return pl.pallas_call(
        kernel_fn,
        out_shape=jax.ShapeDtypeStruct((rows, d), F32),
        grid=(rows // tm,),
        in_specs=[tile] + mixer_specs + [
            _stacked(gains, layer, 4 * which), _stacked(gains, layer, 4 * which + 1),
            _stacked(w, layer, which), _stacked(wout, layer, which)],
        out_specs=tile,
        scratch_shapes=[pltpu.VMEM((tm, wout.shape[-2]), BF16)],
        compiler_params=_params("parallel"),
        name=name,
    )(x, *mixer_args, gains, gains, w, wout)


def _ffn(x, gains, w, wout, layer, which):
    return _ffn_call(_ffn_kernel, "ffn", x, [], [], gains, w, wout, layer, which)


def _mla_out_ffn(x, attn, w_o, j, gains, w, wout, layer, *, tm=512):
    specs = [pl.BlockSpec((tm, attn.shape[1]), lambda i: (i, 0)), _stacked(w_o, j), _stacked(gains, layer, 3)]
    return _ffn_call(_mla_out_ffn_kernel, "mla_out_ffn", x, [attn, w_o, gains], specs, gains, w, wout,
                     layer, 1, tm=tm)


def _hgrn_out_ffn(x, o_f, o_b, gate, out_norm, w_o, j, gains, w, wout, layer, *, tm=512):
    tile = pl.BlockSpec((tm, x.shape[1]), lambda i: (i, 0))
    specs = [tile, tile, tile, _stacked(out_norm, j), _stacked(w_o, j), _stacked(gains, layer, 3)]
    return _ffn_call(_hgrn_out_ffn_kernel, "hgrn_out_ffn", x, [o_f, o_b, gate, out_norm, w_o, gains], specs,
                     gains, w, wout, layer, 1, tm=tm)


def _rope_kernel(pos_ref, invf_ref, cos_ref, sin_ref):
    ang = pos_ref[...].astype(F32) * invf_ref[...]
    cos_ref[...] = jnp.cos(ang)
    sin_ref[...] = jnp.sin(ang)


def _rope_tables(positions):
    rows = positions.size
    half = QK_ROPE_DIM // 2
    per_row = LANES // half
    inv_freq = ROPE_THETA ** (-np.arange(0, QK_ROPE_DIM, 2, dtype=np.float32) / QK_ROPE_DIM)
    invf = jnp.asarray(np.tile(inv_freq, per_row)[None, :])
    pos = jnp.repeat(positions.reshape(rows // per_row, per_row), half, axis=1)
    tm = min(1024, rows // per_row)
    tile = pl.BlockSpec((tm, LANES), lambda i: (i, 0))
    table = jax.ShapeDtypeStruct((rows // per_row, LANES), F32)
    cos, sin = pl.pallas_call(
        _rope_kernel,
        out_shape=(table, table),
        grid=(rows // per_row // tm,),
        in_specs=[tile, _resident((1, LANES))],
        out_specs=(tile, tile),
        compiler_params=_params("parallel"),
        name="rope_tables",
    )(pos, invf)

    def per_token(t):
        t = t.reshape(rows, half)
        return jnp.concatenate([t, t, jnp.zeros((rows, LANES - QK_ROPE_DIM), F32)], axis=1)

    return per_token(cos), per_token(sin)


def _mla_proj_kernel(x_ref, cos_ref, sin_ref, g_ref, win_ref, qn_ref, kvn_ref, wuq_ref, wuqs_ref, wukv_ref,
                     sign_ref, q_ref, k_ref, v_ref, *, q_scale):
    h = _rms(x_ref[...], g_ref[...]).astype(BF16)
    p = jnp.dot(h, win_ref[...], preferred_element_type=F32)
    cq = _rms(p[:, :Q_LORA_RANK], qn_ref[...] * q_scale).astype(BF16)
    ckv = _rms(p[:, Q_LORA_RANK:Q_LORA_RANK + KV_LORA_RANK], kvn_ref[...]).astype(BF16)
    kr = p[:, Q_LORA_RANK + KV_LORA_RANK:Q_LORA_RANK + KV_LORA_RANK + LANES]
    kr_sw = p[:, Q_LORA_RANK + KV_LORA_RANK + LANES:]
    cos = cos_ref[...]
    sin = sin_ref[...] * sign_ref[...]
    k_rope = (kr * cos + kr_sw * sin).astype(BF16)
    q_all = jnp.dot(cq, wuq_ref[...], preferred_element_type=F32)
    q_sw = jnp.dot(cq, wuqs_ref[...], preferred_element_type=F32)
    kv = jnp.dot(ckv, wukv_ref[...], preferred_element_type=F32)
    lane = lax.broadcasted_iota(jnp.int32, (kv.shape[0], LANES), 1)
    ones_col = jnp.where(lane == 0, 1.0, 0.0).astype(BF16)
    for hd in range(MLA_HEADS):
        b0 = hd * QK_PAD_DIM
        q_ref[:, b0:b0 + LANES] = q_all[:, b0:b0 + LANES].astype(BF16)
        q_ref[:, b0 + LANES:b0 + 2 * LANES] = (
            q_all[:, b0 + LANES:b0 + 2 * LANES] * cos + q_sw[:, hd * LANES:(hd + 1) * LANES] * sin
        ).astype(BF16)
        k_ref[:, b0:b0 + LANES] = kv[:, b0:b0 + LANES].astype(BF16)
        k_ref[:, b0 + LANES:b0 + 2 * LANES] = k_rope
        v_ref[:, b0:b0 + LANES] = kv[:, b0 + LANES:b0 + 2 * LANES].astype(BF16)
        v_ref[:, b0 + LANES:b0 + 2 * LANES] = ones_col


def _pad_rope_cols(w, swap):
    half = QK_ROPE_DIM // 2
    a, b = w[..., :half], w[..., half:]
    if swap:
        a, b = b, a
    return jnp.concatenate([a, b, jnp.zeros(w.shape[:-1] + (LANES - QK_ROPE_DIM,), w.dtype)], axis=-1)


def _mla_weights(w_in, w_uq, w_ukv, w_o):
    n = w_in.shape[0]
    lat = Q_LORA_RANK + KV_LORA_RANK
    w_kr = w_in[..., lat:]
    win_ext = jnp.concatenate(
        [w_in[..., :lat], _pad_rope_cols(w_kr, False), _pad_rope_cols(w_kr, True)], axis=-1).astype(BF16)
    wq = w_uq.reshape(n, Q_LORA_RANK, MLA_HEADS, QK_NOPE_DIM + QK_ROPE_DIM)
    rope = wq[..., QK_NOPE_DIM:]
    wuq_ext = jnp.concatenate([wq[..., :QK_NOPE_DIM], _pad_rope_cols(rope, False)], axis=-1)
    wuq_ext = wuq_ext.reshape(n, Q_LORA_RANK, MLA_HEADS * QK_PAD_DIM).astype(BF16)
    wuq_sw = _pad_rope_cols(rope, True).reshape(n, Q_LORA_RANK, MLA_HEADS * LANES).astype(BF16)
    return win_ext, wuq_ext, wuq_sw, w_ukv.astype(BF16), w_o.astype(BF16)


def _mla_proj(x, cos, sin, gains, layer, j, win_ext, q_norm, kv_norm, wuq_ext, wuq_sw, w_ukv, *, tm=512):
    rows, d = x.shape
    half = QK_ROPE_DIM // 2
    sign = np.zeros((1, LANES), np.float32)
    sign[0, :half] = -1.0
    sign[0, half:2 * half] = 1.0
    q_scale = (QK_NOPE_DIM + QK_ROPE_DIM) ** -0.5 * math.log2(math.e)
    hq = MLA_HEADS * QK_PAD_DIM
    out = jax.ShapeDtypeStruct((rows, hq), BF16)
    out_tile = pl.BlockSpec((tm, hq), lambda i: (i, 0))
    return pl.pallas_call(
        functools.partial(_mla_proj_kernel, q_scale=q_scale),
        out_shape=(out, out, out),
        grid=(rows // tm,),
        in_specs=[
            pl.BlockSpec((tm, d), lambda i: (i, 0)),
            pl.BlockSpec((tm, LANES), lambda i: (i, 0)),
            pl.BlockSpec((tm, LANES), lambda i: (i, 0)),
            _stacked(gains, layer, 2),
            _stacked(win_ext, j),
            _stacked(q_norm, j),
            _stacked(kv_norm, j),
            _stacked(wuq_ext, j),
            _stacked(wuq_sw, j),
            _stacked(w_ukv, j),
            _resident((1, LANES)),
        ],
        out_specs=(out_tile, out_tile, out_tile),
        compiler_params=_params("parallel"),
        name="mla_proj",
    )(x, cos, sin, gains, win_ext, q_norm, kv_norm, wuq_ext, wuq_sw, w_ukv, jnp.asarray(sign))


_NT = (((1,), (1,)), ((), ()))


ATTN_SUB_TILES = 4


def _attn_kernel(q_ref, k_ref, v_ref, o_ref, *, tk):
    tq = q_ref.shape[0] // ATTN_SUB_TILES
    n_kv = k_ref.shape[0] // tk
    for sub in range(ATTN_SUB_TILES):
        rows = slice(sub * tq, (sub + 1) * tq)
        q = q_ref[rows, :]

        def scores(j, q=q):
            return lax.dot_general(q, k_ref[j * tk:(j + 1) * tk, :], _NT, preferred_element_type=F32)

        m = jnp.full((tq, 1), -jnp.inf, F32)
        acc = jnp.zeros((tq, 2 * V_HEAD_DIM), F32)
        s_next = scores(0)
        for j in range(n_kv):
            s = s_next
            if j + 1 < n_kv:
                s_next = scores(j + 1)
            m_new = jnp.maximum(m, jnp.max(s, axis=-1, keepdims=True))
            alpha = jnp.exp2(m - m_new)
            p = jnp.exp2(s - m_new).astype(BF16)
            acc = alpha * acc + jnp.dot(p, v_ref[j * tk:(j + 1) * tk, :], preferred_element_type=F32)
            m = m_new
        o_ref[rows, :] = (acc[:, :V_HEAD_DIM] / acc[:, V_HEAD_DIM:V_HEAD_DIM + 1]).astype(o_ref.dtype)


def _attention(q, k, v, batch, seq, *, tq=2048, tk=2048):
    rows = q.shape[0]
    tq = min(tq, seq)
    tk = min(tk, seq)
    nq = seq // tq
    return pl.pallas_call(
        functools.partial(_attn_kernel, tk=tk),
        out_shape=jax.ShapeDtypeStruct((rows, MLA_HEADS * V_HEAD_DIM), BF16),
        grid=(batch, MLA_HEADS, nq),
        in_specs=[
            pl.BlockSpec((tq, QK_PAD_DIM), lambda b, h, i: (b * nq + i, h)),
            pl.BlockSpec((seq, QK_PAD_DIM), lambda b, h, i: (b, h)),
            pl.BlockSpec((seq, 2 * V_HEAD_DIM), lambda b, h, i: (b, h)),
        ],
        out_specs=pl.BlockSpec((tq, V_HEAD_DIM), lambda b, h, i: (b * nq + i, h)),
        compiler_params=_params("parallel", "parallel", "parallel"),
        name="mla_attn",
    )(q, k, v)


HGRN_PROJ_SUB_TILES = 4


def _hgrn_proj_kernel(x_ref, g_ref, w_ref, lbp_ref, qs_ref, hi_ref, lo_ref, kk_ref, v_ref, gate_ref, *,
                      layer_idx):
    d = x_ref.shape[1]
    sub = x_ref.shape[0] // HGRN_PROJ_SUB_TILES
    rows = [slice(s * sub, (s + 1) * sub) for s in range(HGRN_PROJ_SUB_TILES)]
    hs = [_rms(x_ref[r, :], g_ref[...]).astype(BF16) for r in rows]

    def proj(h, j):
        return jnp.dot(h, w_ref[:, j * d:(j + 1) * d], preferred_element_type=F32)

    for r, h in zip(rows, hs):
        qs_ref[r, :] = _silu(proj(h, 0)).astype(BF16)
    for direction in range(2):
        lbp = lbp_ref[direction]
        e = jnp.exp(lbp - jnp.max(lbp, axis=0, keepdims=True))
        sm = e / jnp.sum(e, axis=0, keepdims=True)
        lb = sm[1:2]
        for layer in range(2, layer_idx + 1):
            lb = lb + sm[layer:layer + 1]
        for r, h in zip(rows, hs):
            sg = 1.0 / (1.0 + jnp.exp(-proj(h, 1 + direction)))
            f = lb + (1.0 - lb) * sg
            logf2 = jnp.log2(f)
            hi = logf2.astype(BF16)
            hi_ref[direction, r, :] = hi
            lo_ref[direction, r, :] = (logf2 - hi.astype(F32)).astype(BF16)
            kk_ref[direction, r, :] = (1.0 - f).astype(BF16)
    for r, h in zip(rows, hs):
        v_ref[r, :] = proj(h, 3).astype(BF16)
    for r, h in zip(rows, hs):
        gate_ref[r, :] = _silu(proj(h, 4)).astype(BF16)


def _hgrn_proj(x, gains, layer, j, w_in, lower_bound, *, tm=512):
    rows, d = x.shape
    one = jax.ShapeDtypeStruct((rows, d), BF16)
    two = jax.ShapeDtypeStruct((2, rows, d), BF16)
    spec1 = pl.BlockSpec((tm, d), lambda i: (i, 0))
    spec2 = pl.BlockSpec((2, tm, d), lambda i: (0, i, 0))
    return pl.pallas_call(
        functools.partial(_hgrn_proj_kernel, layer_idx=layer),
        out_shape=(one, two, two, two, one, one),
        grid=(rows // tm,),
        in_specs=[spec1, _stacked(gains, layer, 2), _stacked(w_in, j), _resident(lower_bound.shape)],
        out_specs=(spec1, spec2, spec2, spec2, spec1, spec1),
        compiler_params=_params("parallel"),
        name="hgrn_proj",
    )(x, gains, w_in, lower_bound)


SCAN_EXP_BLOCKS = SCAN_LEVELS + 2
SCAN_TOTAL_ROWS = BF16_ROWS
SCAN_PIPELINE_DEPTH = 4
SCAN_XT_SLOTS = SCAN_LEVELS + 2


def _scan_exponent_matrix():
    c = SCAN_CHUNK
    out = np.zeros((2, SCAN_EXP_BLOCKS * c + SCAN_TOTAL_ROWS, c), np.float32)
    r = np.arange(c)
    for t in range(c):
        for lv in range(SCAN_LEVELS):
            mid = ((t >> lv) | 1) << lv
            if (t >> lv) & 1:
                out[0, lv * c + t, (r >= mid) & (r <= t)] = 1.0
                out[1, lv * c + t, (r >= mid) & (r < t)] = 1.0
            else:
                out[0, lv * c + t, (r > t) & (r < mid)] = 1.0
                out[1, lv * c + t, (r >= t) & (r < mid)] = 1.0
        out[0, SCAN_LEVELS * c + t, r <= t] = 1.0
        out[1, SCAN_LEVELS * c + t, r >= t] = 1.0
        out[0, (SCAN_LEVELS + 1) * c + t, r > t] = 1.0
        out[1, (SCAN_LEVELS + 1) * c + t, r < t] = 1.0
    out[:, SCAN_EXP_BLOCKS * c:, :] = 1.0
    return np.concatenate([out, out], axis=2)


def _scan_prepare(direction, qs_ref, hi_ref, lo_ref, kk_ref, a_ref, xt_scr):
    c = SCAN_CHUNK
    q_bit = 1 - direction
    hl = jnp.concatenate([hi_ref[...], lo_ref[...]], axis=0)
    w = jnp.exp2(jnp.dot(a_ref[direction], hl, preferred_element_type=F32))
    qs = qs_ref[...].astype(F32)
    kk = kk_ref[...].astype(F32)
    n_feat = qs.shape[1]
    xs = []
    for lv in range(SCAN_LEVELS):
        hs = 1 << lv
        if hs >= SUBLANES:
            base = jnp.concatenate(
                [(qs if ((r // hs) & 1) == q_bit else kk)[r:r + hs] for r in range(0, c, hs)], axis=0)
        else:
            sub = lax.broadcasted_iota(jnp.int32, (1, SUBLANES, n_feat), 1)
            pick_q = ((sub >> lv) & 1) == q_bit
            base = jnp.where(pick_q, qs.reshape(c // SUBLANES, SUBLANES, n_feat),
                             kk.reshape(c // SUBLANES, SUBLANES, n_feat)).reshape(c, n_feat)
        xs.append((w[lv * c:(lv + 1) * c] * base).astype(BF16))
        xt_scr[direction, lv] = xs[-1].T
    q_in = (qs * w[SCAN_LEVELS * c:(SCAN_LEVELS + 1) * c]).astype(BF16)
    k_out = (kk * w[(SCAN_LEVELS + 1) * c:(SCAN_LEVELS + 2) * c]).astype(BF16)
    xt_scr[direction, SCAN_LEVELS] = kk_ref[...].T
    xt_scr[direction, SCAN_LEVELS + 1] = k_out.T
    decay = w[SCAN_EXP_BLOCKS * c:SCAN_EXP_BLOCKS * c + SCAN_TOTAL_ROWS].T[:, 0:1]
    return xs, q_in, decay


def _scan_masks(direction):
    c = SCAN_CHUNK
    q_bit = 1 - direction
    t_idx = lax.broadcasted_iota(jnp.int32, (c, c), 0)
    s_idx = lax.broadcasted_iota(jnp.int32, (c, c), 1)
    diff_bits = t_idx ^ s_idx
    causal = (t_idx > s_idx) if direction == 0 else (t_idx < s_idx)
    masks = {"diag": t_idx == s_idx, "full": {}, "q_rows": {}, "q_masks": {}}
    for lv in range(SCAN_LEVELS):
        hs = 1 << lv
        if hs < BF16_ROWS:
            masks["full"][lv] = ((diff_bits >> lv) == 1) & causal
        else:
            rows = [r for r in range(0, c, hs) if ((r // hs) & 1) == q_bit]
            masks["q_rows"][lv] = rows
            masks["q_masks"][lv] = [(diff_bits[r:r + hs] >> lv) == 1 for r in rows]
    return masks


def _scan_level_products(direction, hd, xs, qb, masks, xt_scr):
    c = SCAN_CHUNK
    sl = slice(hd * c, (hd + 1) * c)
    prods = [jnp.dot(qb[:, sl], xt_scr[direction, SCAN_LEVELS, sl, :], preferred_element_type=F32)]
    for lv in range(SCAN_LEVELS):
        x = xs[lv][:, sl]
        xt = xt_scr[direction, lv, sl, :]
        if lv in masks["full"]:
            prods.append(jnp.dot(x, xt, preferred_element_type=F32))
        else:
            hs = 1 << lv
            xq = jnp.concatenate([x[r:r + hs] for r in masks["q_rows"][lv]], axis=0)
            prods.append(jnp.dot(xq, xt, preferred_element_type=F32))
    return prods


def _scan_head_output(direction, hd, prods, masks, v, q_in, decay, o_ref, st_scr, xt_scr):
    c = SCAN_CHUNK
    sl = slice(hd * c, (hd + 1) * c)
    sc = jnp.where(masks["diag"], prods[0], 0.0)
    for lv in range(SCAN_LEVELS):
        g = prods[lv + 1]
        if lv in masks["full"]:
            sc = jnp.where(masks["full"][lv], g, sc)
        else:
            hs = 1 << lv
            rows = masks["q_rows"][lv]
            parts = []
            for r in range(0, c, hs):
                if r in rows:
                    qi = rows.index(r)
                    parts.append(jnp.where(masks["q_masks"][lv][qi], g[qi * hs:(qi + 1) * hs], sc[r:r + hs]))
                else:
                    parts.append(sc[r:r + hs])
            sc = jnp.concatenate(parts, axis=0)
    st = st_scr[direction, hd]
    o = jnp.dot(sc.astype(BF16), v[:, sl], preferred_element_type=F32)
    o = o + jnp.dot(q_in[:, sl], st.astype(BF16), preferred_element_type=F32)
    o_ref[:, sl] = o.astype(o_ref.dtype)
    st_scr[direction, hd] = st * decay[sl] + jnp.dot(xt_scr[direction, SCAN_LEVELS + 1, sl, :], v[:, sl],
                                                     preferred_element_type=F32)


def _hgrn_scan_kernel(qs_f, hi_f, lo_f, kk_f, v_f, qs_b, hi_b, lo_b, kk_b, v_b, a_ref, o_f, o_b,
                      st_scr, xt_scr):
    @pl.when(pl.program_id(1) == 0)
    def _():
        st_scr[...] = jnp.zeros_like(st_scr)

    refs = ((qs_f, hi_f, lo_f, kk_f, v_f, o_f), (qs_b, hi_b, lo_b, kk_b, v_b, o_b))
    prep = [_scan_prepare(d, refs[d][0], refs[d][1], refs[d][2], refs[d][3], a_ref, xt_scr) for d in range(2)]
    masks = [_scan_masks(d) for d in range(2)]
    n_heads = qs_f.shape[1] // SCAN_CHUNK
    units = [(d, hd) for hd in range(n_heads) for d in range(2)]
    pending = []

    def finish(unit, prods):
        d, hd = unit
        _, q_in, decay = prep[d]
        _scan_head_output(d, hd, prods, masks[d], refs[d][4][...], q_in, decay, refs[d][5], st_scr, xt_scr)

    for unit in units:
        d, hd = unit
        pending.append((unit, _scan_level_products(d, hd, prep[d][0], refs[d][0][...], masks[d], xt_scr)))
        if len(pending) > SCAN_PIPELINE_DEPTH:
            finish(*pending.pop(0))
    while pending:
        finish(*pending.pop(0))


def _hgrn_scan(qs, hi, lo, kk, v, batch, seq):
    rows, f = qs.shape
    c = SCAN_CHUNK
    nc = seq // c
    a_mat = jnp.asarray(_scan_exponent_matrix(), dtype=BF16)

    def fwd(b, j):
        return b * nc + j

    def bwd(b, j):
        return b * nc + nc - 1 - j

    def specs(chunk, direction):
        one = pl.BlockSpec((c, f), lambda b, j: (chunk(b, j), 0))
        two = pl.BlockSpec((None, c, f), lambda b, j: (direction, chunk(b, j), 0))
        return [one, two, two, two, one]

    out = jax.ShapeDtypeStruct((rows, f), BF16)
    return pl.pallas_call(
        _hgrn_scan_kernel,
        out_shape=(out, out),
        grid=(batch, nc),
        in_specs=specs(fwd, 0) + specs(bwd, 1) + [_resident(a_mat.shape)],
        out_specs=(pl.BlockSpec((c, f), lambda b, j: (fwd(b, j), 0)),
                   pl.BlockSpec((c, f), lambda b, j: (bwd(b, j), 0))),
        scratch_shapes=[pltpu.VMEM((2, HGRN_HEADS, c, c), F32),
                        pltpu.VMEM((2, SCAN_XT_SLOTS, f, c), BF16)],
        compiler_params=_params("parallel", "arbitrary"),
        name="hgrn_scan",
    )(qs, hi, lo, kk, v, qs, hi, lo, kk, v, a_mat)


def kernel(x, positions, norm_gains, ffn_w_in, ffn_w_out, mla_w_in, mla_q_norm, mla_kv_norm, mla_w_uq,
           mla_w_ukv, mla_w_o, hgrn_w_in, hgrn_lower_bound, hgrn_out_norm, hgrn_w_o):
    batch, seq, d = x.shape
    depth = norm_gains.shape[0]
    rows = batch * seq
    xf = x.reshape(rows, d)
    cos, sin = _rope_tables(positions)
    gains = norm_gains.reshape(depth, N_NORMS_PER_LAYER, 1, d)
    ffn_w = ffn_w_in.astype(BF16)
    wout = ffn_w_out.astype(BF16)
    win_ext, wuq_ext, wuq_sw, wukv, mla_wo = _mla_weights(mla_w_in, mla_w_uq, mla_w_ukv, mla_w_o)
    q_norm = mla_q_norm.reshape(-1, 1, Q_LORA_RANK)
    kv_norm = mla_kv_norm.reshape(-1, 1, KV_LORA_RANK)
    hgrn_win = hgrn_w_in.astype(BF16)
    hgrn_wo = hgrn_w_o.astype(BF16)
    out_norm = hgrn_out_norm.reshape(-1, 1, d)
    for l in range(depth):
        xf = _ffn(xf, gains, ffn_w, wout, l, 0)
        j = l // N_MIXERS
        if l % N_MIXERS == 0:
            q, k, v = _mla_proj(xf, cos, sin, gains, l, j, win_ext, q_norm, kv_norm, wuq_ext, wuq_sw, wukv)
            o = _attention(q, k, v, batch, seq)
            xf = _mla_out_ffn(xf, o, mla_wo, j, gains, ffn_w, wout, l)
        else:
            qs, hi, lo, kk, v, gate = _hgrn_proj(xf, gains, l, j, hgrn_win, hgrn_lower_bound)
            o_f, o_b = _hgrn_scan(qs, hi, lo, kk, v, batch, seq)
            xf = _hgrn_out_ffn(xf, o_f, o_b, gate, out_norm, hgrn_wo, j, gains, ffn_w, wout, l)
    return xf.reshape(batch, seq, d)
```
